```python
import math
import jax
import jax.numpy as jnp
from jax import lax
import numpy as np

D_MODEL = 1024
BATCH = 4
SEQ = 8192
DEPTH = 1
DEC_BATCH = 16
DEC_SEQ = 32
PAST_LEN = 1024

CHUNK = 64
Q_BLOCK = 128
SB_HEADS = 8
SB_HEAD_DIM = D_MODEL // 16
SB_WIDTH = SB_HEADS * SB_HEAD_DIM
SB_SCALE = 1.0 / math.sqrt(SB_HEAD_DIM)
MLA_HEADS = 8
MLA_NOPE_DIM = D_MODEL // 16
MLA_ROPE_DIM = D_MODEL // 32
MLA_V_DIM = D_MODEL // 16
MLA_Q_LORA = D_MODEL // 4
MLA_KV_LORA = D_MODEL // 8
MLA_WIDTH = MLA_HEADS * MLA_V_DIM
MLA_SCALE = 1.0 / math.sqrt(MLA_NOPE_DIM + MLA_ROPE_DIM)
ROPE_BASE = 10000.0
N_GROUPS = 4
EXPERTS_PER_GROUP = 8
N_EXPERTS = N_GROUPS * EXPERTS_PER_GROUP
TOP_K_IN_GROUP = 2
D_EXPERT = D_MODEL // 4
RMS_EPS = 1e-6
IN_COLS = 3 * SB_WIDTH + MLA_Q_LORA + MLA_KV_LORA + MLA_ROPE_DIM + 2 * D_MODEL

kernel_name = 'stickbreak_mla_hmoe_stream_step'


def _rmsnorm(x, g):
    xf = x.astype(jnp.float32)
    y = xf * lax.rsqrt(jnp.mean(xf * xf, axis=-1, keepdims=True) + RMS_EPS)
    return (y * g.astype(jnp.float32)).astype(x.dtype)


def _rope(x, pos):
    half = MLA_ROPE_DIM // 2
    inv_freq = ROPE_BASE ** (-jnp.arange(half, dtype=jnp.float32) / half)
    ang = pos.astype(jnp.float32)[:, None] * inv_freq[None, :]
    cos = jnp.cos(ang)[None, :, None, :]
    sin = jnp.sin(ang)[None, :, None, :]
    xf = x.astype(jnp.float32)
    x1, x2 = xf[..., :half], xf[..., half:]
    return jnp.concatenate([x1 * cos - x2 * sin, x1 * sin + x2 * cos], axis=-1).astype(x.dtype)


def _mixer_inputs(xn, pos, w_in, q_norm, w_q_up, kv_norm, w_uk):
    B, T, _ = xn.shape
    proj = jnp.einsum('btd,dc->btc', xn, w_in)
    sizes = [SB_WIDTH, SB_WIDTH, SB_WIDTH, MLA_Q_LORA, MLA_KV_LORA, MLA_ROPE_DIM, D_MODEL, D_MODEL]
    cuts = np.cumsum(sizes)[:-1].tolist()
    q_sb, k_sb, v_sb, cq, ckv, kpe, g_sb, g_mla = jnp.split(proj, cuts, axis=-1)
    to_heads = lambda a: a.reshape(B, T, SB_HEADS, SB_HEAD_DIM)
    q_full = jnp.einsum('btc,chd->bthd', _rmsnorm(cq, q_norm), w_q_up)
    q_nope, q_pe = q_full[..., :MLA_NOPE_DIM], q_full[..., MLA_NOPE_DIM:]
    q_lat = jnp.einsum('bthd,chd->bthc', q_nope, w_uk)
    q_pe = _rope(q_pe, pos)
    ckv = _rmsnorm(ckv, kv_norm)
    kpe = _rope(kpe[:, :, None, :], pos)[:, :, 0, :]
    return to_heads(q_sb), to_heads(k_sb), to_heads(v_sb), q_lat, q_pe, ckv, kpe, g_sb, g_mla


def _stick_breaking(q, k, v, q_pos, k_pos):
    z = jnp.einsum('bqhd,bkhd->bhqk', q, k).astype(jnp.float32) * SB_SCALE
    mask = k_pos[None, :] < q_pos[:, None]
    log_beta = jax.nn.log_sigmoid(z)
    log_one_minus = jnp.where(mask, jax.nn.log_sigmoid(-z), 0.0)
    suffix = lax.cumsum(log_one_minus, axis=3, reverse=True) - log_one_minus
    weights = jnp.where(mask, jnp.exp(log_beta + suffix), 0.0)
    out = jnp.einsum('bhqk,bkhd->bqhd', weights, v.astype(jnp.float32))
    return out.astype(v.dtype)


def _mla_attend(q_lat, q_pe, ckv, kpe, q_pos, k_pos):
    s = (jnp.einsum('bqhc,bkc->bhqk', q_lat, ckv)
         + jnp.einsum('bqhr,bkr->bhqk', q_pe, kpe)).astype(jnp.float32) * MLA_SCALE
    mask = (k_pos[None, :] // CHUNK) <= (q_pos[:, None] // CHUNK)
    p = jax.nn.softmax(jnp.where(mask, s, -jnp.inf), axis=-1)
    out = jnp.einsum('bhqk,bkc->bqhc', p, ckv.astype(jnp.float32))
    return out.astype(ckv.dtype)


def _sweep_query_blocks(fn, q_arrays, q_pos):
    n_q = q_pos.shape[0]
    n_blocks = n_q // Q_BLOCK

    def to_blocks(a):
        a = a.reshape(a.shape[0], n_blocks, Q_BLOCK, *a.shape[2:])
        return jnp.moveaxis(a, 1, 0)

    blocks = tuple(to_blocks(a) for a in q_arrays) + (q_pos.reshape(n_blocks, Q_BLOCK),)
    out = lax.map(lambda args: fn(*args), blocks)
    out = jnp.moveaxis(out, 0, 1)
    return out.reshape(out.shape[0], n_q, *out.shape[3:])


def _merge(sb_o, mla_lat, g_sb, g_mla, w_uv, w_sb_branch, w_mla_branch, w_out):
    B, T = sb_o.shape[:2]
    mla_o = jnp.einsum('bthc,chd->bthd', mla_lat, w_uv).reshape(B, T, MLA_WIDTH)
    sb_b = jnp.einsum('btc,cd->btd', sb_o.reshape(B, T, SB_WIDTH), w_sb_branch)
    mla_b = jnp.einsum('btc,cd->btd', mla_o, w_mla_branch)
    merged = jax.nn.sigmoid(g_sb) * sb_b + jax.nn.sigmoid(g_mla) * mla_b
    return jnp.einsum('btd,de->bte', merged, w_out)


def _moe_block(h, ln2, w_group, b_group, w_router, b_router, w_gate, w_up, w_down):
    B, T, D = h.shape
    xf = _rmsnorm(h, ln2).reshape(B * T, D)
    group_logits = (xf @ w_group + b_group).astype(jnp.float32)
    g_idx = jnp.argmax(group_logits, axis=-1)
    g_val = jnp.take_along_axis(jax.nn.softmax(group_logits, axis=-1), g_idx[:, None], axis=-1)[:, 0]
    expert_logits = (xf @ w_router + b_router).astype(jnp.float32).reshape(-1, N_GROUPS, EXPERTS_PER_GROUP)
    in_group = jnp.take_along_axis(expert_logits, g_idx[:, None, None], axis=1)[:, 0]
    top_vals, top_idx = lax.top_k(in_group, TOP_K_IN_GROUP)
    top_w = jax.nn.softmax(top_vals, axis=-1) * g_val[:, None]
    w_in_group = jnp.sum(jax.nn.one_hot(top_idx, EXPERTS_PER_GROUP, dtype=jnp.float32) * top_w[..., None], axis=1)
    combine = (jax.nn.one_hot(g_idx, N_GROUPS, dtype=jnp.float32)[:, :, None]
               * w_in_group[:, None, :]).astype(xf.dtype)
    wg = w_gate.reshape(N_GROUPS, EXPERTS_PER_GROUP, D, D_EXPERT)
    wu = w_up.reshape(N_GROUPS, EXPERTS_PER_GROUP, D, D_EXPERT)
    wd = w_down.reshape(N_GROUPS, EXPERTS_PER_GROUP, D_EXPERT, D)
    out = jnp.zeros_like(xf)
    for g in range(N_GROUPS):
        act = jax.nn.silu(jnp.einsum('nd,edf->nef', xf, wg[g])) * jnp.einsum('nd,edf->nef', xf, wu[g])
        out = out + jnp.einsum('nef,efd->nd', act * combine[:, g, :, None], wd[g])
    return h + out.reshape(B, T, D)


def setup_inputs(seed: int = 0) -> dict:
    key = jax.random.key(seed)
    ks = jax.random.split(key, 32)
    f32 = jnp.float32
    nrm = lambda k, shape, scale: jax.random.normal(k, shape, f32) * scale
    gain = lambda k, shape: 1.0 + 0.01 * jax.random.normal(k, shape, f32)
    L = DEPTH
    return {
        'x_prompt': nrm(ks[0], (BATCH, SEQ, D_MODEL), 1.0),
        'x_sample': nrm(ks[1], (DEC_BATCH, DEC_SEQ, D_MODEL), 1.0),
        'cache_sb_k': nrm(ks[2], (L, DEC_BATCH, PAST_LEN, SB_HEADS, SB_HEAD_DIM), 1.0),
        'cache_sb_v': nrm(ks[3], (L, DEC_BATCH, PAST_LEN, SB_HEADS, SB_HEAD_DIM), 1.0),
        'cache_mla_ckv': nrm(ks[4], (L, DEC_BATCH, PAST_LEN, MLA_KV_LORA), 1.0),
        'cache_mla_kpe': nrm(ks[5], (L, DEC_BATCH, PAST_LEN, MLA_ROPE_DIM), 1.0),
        'ln1': gain(ks[6], (L, D_MODEL)),
        'w_in': nrm(ks[7], (L, D_MODEL, IN_COLS), D_MODEL ** -0.5),
        'q_norm': gain(ks[8], (L, MLA_Q_LORA)),
        'w_q_up': nrm(ks[9], (L, MLA_Q_LORA, MLA_HEADS, MLA_NOPE_DIM + MLA_ROPE_DIM), MLA_Q_LORA ** -0.5),
        'kv_norm': gain(ks[10], (L, MLA_KV_LORA)),
        'w_uk': nrm(ks[11], (L, MLA_KV_LORA, MLA_HEADS, MLA_NOPE_DIM), MLA_KV_LORA ** -0.5),
        'w_uv': nrm(ks[12], (L, MLA_KV_LORA, MLA_HEADS, MLA_V_DIM), MLA_KV_LORA ** -0.5),
        'w_sb_branch': nrm(ks[13], (L, SB_WIDTH, D_MODEL), SB_WIDTH ** -0.5),
        'w_mla_branch': nrm(ks[14], (L, MLA_WIDTH, D_MODEL), MLA_WIDTH ** -0.5),
        'w_out': nrm(ks[15], (L, D_MODEL, D_MODEL), D_MODEL ** -0.5),
        'ln2': gain(ks[16], (L, D_MODEL)),
        'w_group': nrm(ks[17], (L, D_MODEL, N_GROUPS), D_MODEL ** -0.5),
        'b_group': nrm(ks[18], (L, N_GROUPS), 0.01),
        'w_router': nrm(ks[19], (L, D_MODEL, N_EXPERTS), D_MODEL ** -0.5),
        'b_router': nrm(ks[20], (L, N_EXPERTS), 0.01),
        'w_gate': nrm(ks[21], (L, N_EXPERTS, D_MODEL, D_EXPERT), D_MODEL ** -0.5),
        'w_up': nrm(ks[22], (L, N_EXPERTS, D_MODEL, D_EXPERT), D_MODEL ** -0.5),
        'w_down': nrm(ks[23], (L, N_EXPERTS, D_EXPERT, D_MODEL), D_EXPERT ** -0.5),
        'ln_f': gain(ks[24], (D_MODEL,)),
    }


def reference(x_prompt, x_sample, cache_sb_k, cache_sb_v, cache_mla_ckv, cache_mla_kpe,
              ln1, w_in, q_norm, w_q_up, kv_norm, w_uk, w_uv, w_sb_branch, w_mla_branch, w_out,
              ln2, w_group, b_group, w_router, b_router, w_gate, w_up, w_down, ln_f):
    seq = x_prompt.shape[1]
    dec_seq = x_sample.shape[1]
    past = cache_sb_k.shape[2]
    pos_p = jnp.arange(seq, dtype=jnp.int32)
    pos_s = past + jnp.arange(dec_seq, dtype=jnp.int32)
    kpos_s = jnp.arange(past + dec_seq, dtype=jnp.int32)

    xp, xs = x_prompt, x_sample
    nk_p, nv_p, nc_p, ne_p = [], [], [], []
    nk_s, nv_s, nc_s, ne_s = [], [], [], []
    for l in range(DEPTH):
        xn = _rmsnorm(xp, ln1[l])
        q, k, v, q_lat, q_pe, ckv, kpe, g_sb, g_mla = _mixer_inputs(
            xn, pos_p, w_in[l], q_norm[l], w_q_up[l], kv_norm[l], w_uk[l])
        sb_o = _sweep_query_blocks(
            lambda qb, pb: _stick_breaking(qb, k, v, pb, pos_p), (q,), pos_p)
        mla_lat = _sweep_query_blocks(
            lambda qlb, qpb, pb: _mla_attend(qlb, qpb, ckv, kpe, pb, pos_p), (q_lat, q_pe), pos_p)
        h = xp + _merge(sb_o, mla_lat, g_sb, g_mla, w_uv[l], w_sb_branch[l], w_mla_branch[l], w_out[l])
        xp = _moe_block(h, ln2[l], w_group[l], b_group[l], w_router[l], b_router[l],
                        w_gate[l], w_up[l], w_down[l])
        nk_p.append(k)
        nv_p.append(v)
        nc_p.append(ckv)
        ne_p.append(kpe)

        xn_s = _rmsnorm(xs, ln1[l])
        q2, k2, v2, q_lat2, q_pe2, ckv2, kpe2, g_sb2, g_mla2 = _mixer_inputs(
            xn_s, pos_s, w_in[l], q_norm[l], w_q_up[l], kv_norm[l], w_uk[l])
        k_all = jnp.concatenate([cache_sb_k[l], k2], axis=1)
        v_all = jnp.concatenate([cache_sb_v[l], v2], axis=1)
        ckv_all = jnp.concatenate([cache_mla_ckv[l], ckv2], axis=1)
        kpe_all = jnp.concatenate([cache_mla_kpe[l], kpe2], axis=1)
        sb_o2 = _stick_breaking(q2, k_all, v_all, pos_s, kpos_s)
        mla_lat2 = _mla_attend(q_lat2, q_pe2, ckv_all, kpe_all, pos_s, kpos_s)
        h2 = xs + _merge(sb_o2, mla_lat2, g_sb2, g_mla2, w_uv[l], w_sb_branch[l], w_mla_branch[l], w_out[l])
        xs = _moe_block(h2, ln2[l], w_group[l], b_group[l], w_router[l], b_router[l],
                        w_gate[l], w_up[l], w_down[l])
        nk_s.append(k2)
        nv_s.append(v2)
        nc_s.append(ckv2)
        ne_s.append(kpe2)

    y_prompt = _rmsnorm(xp, ln_f)
    y_sample = _rmsnorm(xs, ln_f)
    return (y_prompt, y_sample,
            jnp.stack(nk_p), jnp.stack(nv_p), jnp.stack(nc_p), jnp.stack(ne_p),
            jnp.stack(nk_s), jnp.stack(nv_s), jnp.stack(nc_s), jnp.stack(ne_s))
```

```python
import functools
import math

import jax
import jax.numpy as jnp
from jax import lax
from jax.experimental import pallas as pl
from jax.experimental.pallas import tpu as pltpu

F32 = jnp.float32
BF16 = jnp.bfloat16

CHUNK = 64
CHUNK_SHIFT = CHUNK.bit_length() - 1
assert 1 << CHUNK_SHIFT == CHUNK
SB_HEADS = 8
MLA_HEADS = 8
ROPE_BASE = 10000.0
N_GROUPS = 4
EXPERTS_PER_GROUP = 8
N_EXPERTS = N_GROUPS * EXPERTS_PER_GROUP
RMS_EPS = 1e-6

LANES = 128
SB_KBLK = LANES
SB_DEAD_CARRY = -110.0
VMEM_LIMIT = 56 * 1024 * 1024
ROUTER_LANE0 = N_GROUPS


def _dot(a, b):
    return jnp.dot(a, b, preferred_element_type=F32)


def _dot_nt(a, b):
    return lax.dot_general(a, b, (((1,), (1,)), ((), ())), preferred_element_type=F32)


def _rms(x, g):
    return x * lax.rsqrt(jnp.mean(x * x, axis=-1, keepdims=True) + RMS_EPS) * g


def _params(sem):
    return pltpu.CompilerParams(dimension_semantics=sem, vmem_limit_bytes=VMEM_LIMIT)


def _const_spec(shape):
    return pl.BlockSpec(shape, lambda *_: (0,) * len(shape))


def _proj_kernel(x_ref, cos_ref, sin_ref, ln1_ref, wmain_ref, qn_ref, kvn_ref, wqn_ref, wuk_ref, wqpe_ref, wqpes_ref,
                 q_ref, k_ref, v_ref, kb_ref, vb_ref, ckv_ref, kpe_ref, kcat_ref, qcat_ref, *, sb_scale, mla_scale, sbw, qlora,
                 kvlora, rope):
    xn = _rms(x_ref[0], ln1_ref[...]).astype(BF16)
    p = _dot(xn, wmain_ref[...])
    o = 0
    q = p[:, o:o + sbw]; o += sbw
    k = p[:, o:o + sbw]; o += sbw
    v = p[:, o:o + sbw]; o += sbw
    cq = p[:, o:o + qlora]; o += qlora
    ckv = p[:, o:o + kvlora]; o += kvlora
    kpe = p[:, o:o + LANES]; o += LANES
    kpe_sw = p[:, o:o + LANES]
    q_ref[0] = (q * sb_scale).astype(BF16)
    k_ref[0] = k
    v_ref[0] = v
    kb_ref[0] = k.astype(BF16)
    vb_ref[0] = v.astype(BF16)
    cos = cos_ref[...]
    sin = sin_ref[...]
    ckvn = _rms(ckv, kvn_ref[...])
    ckv_ref[0] = ckvn
    kpe_rot = kpe * cos + kpe_sw * sin
    kpe_ref[0] = kpe_rot[:, :rope]
    kcat_ref[0] = jnp.concatenate([ckvn, kpe_rot], axis=-1).astype(BF16)
    cqn = _rms(cq, qn_ref[...]).astype(BF16)
    qnope = _dot(cqn, wqn_ref[...]).astype(BF16)
    qlat = _dot(qnope, wuk_ref[...])
    qpe = _dot(cqn, wqpe_ref[...])
    qpe_sw = _dot(cqn, wqpes_ref[...])
    for h in range(MLA_HEADS):
        sl = slice(h * LANES, (h + 1) * LANES)
        qcat_ref[0, h, :, :kvlora] = (qlat[:, h * kvlora:(h + 1) * kvlora] * mla_scale).astype(BF16)
        qcat_ref[0, h, :, kvlora:] = ((qpe[:, sl] * cos + qpe_sw[:, sl] * sin) * mla_scale).astype(BF16)


def _proj(x, cos_t, sin_t, ln1, wmain, qn, kvn, wqn, wuk, wqpe, wqpes, *, tm, sb_scale, mla_scale, sbw, qlora, kvlora, rope):
    B, T, D = x.shape
    grid = (B, T // tm)
    tok = lambda w: pl.BlockSpec((1, tm, w), lambda b, i: (b, i, 0))
    tab = pl.BlockSpec((tm, LANES), lambda b, i: (i, 0))
    kcw = kvlora + LANES
    out_shape = (
        jax.ShapeDtypeStruct((B, T, sbw), BF16),
        jax.ShapeDtypeStruct((B, T, sbw), F32),
        jax.ShapeDtypeStruct((B, T, sbw), F32),
        jax.ShapeDtypeStruct((B, T, sbw), BF16),
        jax.ShapeDtypeStruct((B, T, sbw), BF16),
        jax.ShapeDtypeStruct((B, T, kvlora), F32),
        jax.ShapeDtypeStruct((B, T, rope), F32),
        jax.ShapeDtypeStruct((B, T, kcw), BF16),
        jax.ShapeDtypeStruct((B, MLA_HEADS, T, kcw), BF16),
    )
    out_specs = (tok(sbw), tok(sbw), tok(sbw), tok(sbw), tok(sbw), tok(kvlora), tok(rope), tok(kcw),
                 pl.BlockSpec((1, MLA_HEADS, tm, kcw), lambda b, i: (b, 0, i, 0)))
    in_specs = [tok(D), tab, tab] + [_const_spec(a.shape) for a in (ln1, wmain, qn, kvn, wqn, wuk, wqpe, wqpes)]
    kern = functools.partial(_proj_kernel, sb_scale=sb_scale, mla_scale=mla_scale, sbw=sbw, qlora=qlora, kvlora=kvlora,
                             rope=rope)
    return pl.pallas_call(kern, out_shape=out_shape, grid=grid, in_specs=in_specs, out_specs=out_specs,
                          compiler_params=_params(("parallel", "parallel")), name="proj")(
        x, cos_t, sin_t, ln1, wmain, qn, kvn, wqn, wuk, wqpe, wqpes)


def _head_pair_blockdiag(blk):
    lane = lax.broadcasted_iota(jnp.int32, blk.shape, 1)
    zero = jnp.zeros_like(blk)
    half = LANES // 2
    return jnp.concatenate([jnp.where(lane < half, blk, zero), jnp.where(lane >= half, blk, zero)], axis=0)


def _sb_step(q, kblk, vblk, tri, carry, acc, qpos0, kpos0, masked):
    tq = q.shape[0]
    z = _dot_nt(q, _head_pair_blockdiag(kblk))
    lb = jnp.minimum(z, 0.0) - jnp.log1p(jnp.exp(-jnp.abs(z)))
    lom = lb - z
    if masked:
        row = lax.broadcasted_iota(jnp.int32, (tq, 2 * SB_KBLK), 0)
        col = lax.broadcasted_iota(jnp.int32, (tq, 2 * SB_KBLK), 1)
        keep = (kpos0 + (col & (SB_KBLK - 1))) < (qpos0 + row)
        lom = jnp.where(keep, lom, 0.0)
    hi = lom.astype(BF16)
    lo = (lom - hi.astype(F32)).astype(BF16)
    r = _dot(hi, tri) + _dot(lo, tri)
    suffix = r[:, :2 * SB_KBLK] + carry
    w = jnp.exp(lb + suffix)
    if masked:
        w = jnp.where(keep, w, 0.0)
    acc = acc + _dot(w.astype(BF16), _head_pair_blockdiag(vblk))
    carry = carry + r[:, 2 * SB_KBLK:]
    return carry, acc


def _sb_prompt_kernel(q_ref, k_ref, v_ref, tri_ref, o_ref, carry_ref, acc_ref, *, tq):
    i = pl.program_id(2)
    q = q_ref[0]
    tri = tri_ref[...]
    nk = tq // SB_KBLK
    qpos0 = i * tq
    carry = jnp.zeros((tq, 2 * SB_KBLK), F32)
    acc = jnp.zeros((tq, LANES), F32)
    for d in range(nk):
        j = i * nk + (nk - 1 - d)
        ks = pl.multiple_of(j * SB_KBLK, SB_KBLK)
        carry, acc = _sb_step(q, k_ref[0, pl.ds(ks, SB_KBLK), :], v_ref[0, pl.ds(ks, SB_KBLK), :], tri, carry, acc,
                              qpos0, j * SB_KBLK, True)
    carry_ref[...] = carry
    acc_ref[...] = acc

    def cond(s):
        j, live = s
        return jnp.logical_and(j >= 0, live > SB_DEAD_CARRY)

    def body(s):
        j, _ = s
        ks = pl.multiple_of(j * SB_KBLK, SB_KBLK)
        c, a = _sb_step(q, k_ref[0, pl.ds(ks, SB_KBLK), :], v_ref[0, pl.ds(ks, SB_KBLK), :], tri, carry_ref[...],
                        acc_ref[...], qpos0, 0, False)
        carry_ref[...] = c
        acc_ref[...] = a
        return j - 1, jnp.max(c)

    lax.while_loop(cond, body, (i * nk - 1, jnp.max(carry)))
    o_ref[0] = acc_ref[...].astype(o_ref.dtype)


def _sb_prompt(q, kb, vb, tri, *, tq):
    B, T, W = q.shape
    grid = (B, W // LANES, T // tq)
    qspec = pl.BlockSpec((1, tq, LANES), lambda b, p, i: (b, i, p))
    kspec = pl.BlockSpec((1, T, LANES), lambda b, p, i: (b, 0, p))
    return pl.pallas_call(
        functools.partial(_sb_prompt_kernel, tq=tq), out_shape=jax.ShapeDtypeStruct((B, T, W), BF16), grid=grid,
        in_specs=[qspec, kspec, kspec, _const_spec(tri.shape)], out_specs=qspec,
        scratch_shapes=[pltpu.VMEM((tq, 2 * SB_KBLK), F32), pltpu.VMEM((tq, LANES), F32)],
        compiler_params=_params(("parallel", "parallel", "arbitrary")), name="sb_prompt")(q, kb, vb, tri)


def _sb_sample_kernel(q_ref, k_ref, v_ref, tri_ref, o_ref, *, tq, nkb, qpos0):
    q = q_ref[0]
    tri = tri_ref[...]
    carry = jnp.zeros((tq, 2 * SB_KBLK), F32)
    acc = jnp.zeros((tq, LANES), F32)
    for j in range(nkb - 1, -1, -1):
        ks = j * SB_KBLK
        carry, acc = _sb_step(q, k_ref[0, ks:ks + SB_KBLK, :], v_ref[0, ks:ks + SB_KBLK, :], tri, carry, acc, qpos0, ks, True)
    o_ref[0] = acc.astype(o_ref.dtype)


def _sb_sample(q, k_all, v_all, tri, *, qpos0):
    B, tq, W = q.shape
    Tk = k_all.shape[1]
    grid = (B, W // LANES)
    qspec = pl.BlockSpec((1, tq, LANES), lambda b, p: (b, 0, p))
    kspec = pl.BlockSpec((1, Tk, LANES), lambda b, p: (b, 0, p))
    return pl.pallas_call(
        functools.partial(_sb_sample_kernel, tq=tq, nkb=Tk // SB_KBLK, qpos0=qpos0),
        out_shape=jax.ShapeDtypeStruct((B, tq, W), BF16), grid=grid,
        in_specs=[qspec, kspec, kspec, _const_spec(tri.shape)], out_specs=qspec,
        compiler_params=_params(("parallel", "parallel")), name="sb_sample")(q, k_all, v_all, tri)


def _mla_prompt_kernel(q_ref, kc_ref, o_ref, m_ref, l_ref, acc_ref, *, tq, tk, kvlora):
    i = pl.program_id(1)
    H = q_ref.shape[1]
    q = q_ref[0].reshape(H * tq, q_ref.shape[3])
    m_ref[...] = jnp.full(m_ref.shape, -jnp.inf, F32)
    l_ref[...] = jnp.zeros(l_ref.shape, F32)
    acc_ref[...] = jnp.zeros(acc_ref.shape, F32)

    def step(j, masked):
        ks = pl.multiple_of(j * tk, tk)
        kblk = kc_ref[0, pl.ds(ks, tk), :]
        s = _dot_nt(q, kblk)
        if masked:
            row = lax.broadcasted_iota(jnp.int32, s.shape, 0)
            col = lax.broadcasted_iota(jnp.int32, s.shape, 1)
            qpos = i * tq + (row & (tq - 1))
            kpos = j * tk + col
            s = jnp.where((kpos >> CHUNK_SHIFT) <= (qpos >> CHUNK_SHIFT), s, -jnp.inf)
        m_old = m_ref[...]
        m_new = jnp.maximum(m_old, jnp.max(s, axis=-1, keepdims=True))
        alpha = jnp.exp(m_old - m_new)
        p = jnp.exp(s - m_new)
        l_ref[...] = alpha * l_ref[...] + jnp.sum(p, axis=-1, keepdims=True)
        acc_ref[...] = alpha * acc_ref[...] + _dot(p.astype(BF16), kblk[:, :kvlora])
        m_ref[...] = m_new

    nfull = (i * tq) // tk

    def body(j, c):
        step(j, False)
        return c

    lax.fori_loop(0, nfull, body, 0)
    for d in range(tq // tk):
        step(nfull + d, True)
    out = acc_ref[...] / l_ref[...]
    for h in range(H):
        o_ref[0, :, h * kvlora:(h + 1) * kvlora] = out[h * tq:(h + 1) * tq].astype(o_ref.dtype)


def _mla_prompt(qcat, kcat, *, tq, tk, kvlora):
    B, H, T, KC = qcat.shape
    grid = (B, T // tq)
    return pl.pallas_call(
        functools.partial(_mla_prompt_kernel, tq=tq, tk=tk, kvlora=kvlora),
        out_shape=jax.ShapeDtypeStruct((B, T, H * kvlora), BF16), grid=grid,
        in_specs=[pl.BlockSpec((1, H, tq, KC), lambda b, i: (b, 0, i, 0)), pl.BlockSpec((1, T, KC), lambda b, i: (b, 0, 0))],
        out_specs=pl.BlockSpec((1, tq, H * kvlora), lambda b, i: (b, i, 0)),
        scratch_shapes=[pltpu.VMEM((H * tq, 1), F32), pltpu.VMEM((H * tq, 1), F32), pltpu.VMEM((H * tq, kvlora), F32)],
        compiler_params=_params(("parallel", "arbitrary")), name="mla_prompt")(qcat, kcat)


def _mla_sample_kernel(q_ref, kc_ref, o_ref, *, tq, qpos0, nvalid, kvlora):
    H = q_ref.shape[1]
    q = q_ref[0].reshape(H * tq, q_ref.shape[3])
    kall = kc_ref[0]
    s = _dot_nt(q, kall)
    row = lax.broadcasted_iota(jnp.int32, s.shape, 0)
    col = lax.broadcasted_iota(jnp.int32, s.shape, 1)
    qpos = qpos0 + (row & (tq - 1))
    keep = jnp.logical_and((col >> CHUNK_SHIFT) <= (qpos >> CHUNK_SHIFT), col < nvalid)
    s = jnp.where(keep, s, -jnp.inf)
    p = jnp.exp(s - jnp.max(s, axis=-1, keepdims=True))
    out = _dot(p.astype(BF16), kall[:, :kvlora]) / jnp.sum(p, axis=-1, keepdims=True)
    for h in range(H):
        o_ref[0, :, h * kvlora:(h + 1) * kvlora] = out[h * tq:(h + 1) * tq].astype(o_ref.dtype)


def _mla_sample(qcat, kcat_all, *, qpos0, nvalid, kvlora):
    B, H, tq, KC = qcat.shape
    Tk = kcat_all.shape[1]
    return pl.pallas_call(
        functools.partial(_mla_sample_kernel, tq=tq, qpos0=qpos0, nvalid=nvalid, kvlora=kvlora),
        out_shape=jax.ShapeDtypeStruct((B, tq, H * kvlora), BF16), grid=(B,),
        in_specs=[pl.BlockSpec((1, H, tq, KC), lambda b: (b, 0, 0, 0)), pl.BlockSpec((1, Tk, KC), lambda b: (b, 0, 0))],
        out_specs=pl.BlockSpec((1, tq, H * kvlora), lambda b: (b, 0, 0)),
        compiler_params=_params(("parallel",)), name="mla_sample")(qcat, kcat_all)


def _route(logits):
    lane = lax.broadcasted_iota(jnp.int32, logits.shape, 1)
    ninf = -jnp.inf
    gl = jnp.where(lane < N_GROUPS, logits, ninf)
    gmax = jnp.max(gl, axis=-1, keepdims=True)
    gidx = jnp.min(jnp.where(gl == gmax, lane, LANES), axis=-1, keepdims=True)
    g_val = 1.0 / jnp.sum(jnp.where(lane < N_GROUPS, jnp.exp(logits - gmax), 0.0), axis=-1, keepdims=True)
    lo = ROUTER_LANE0 + EXPERTS_PER_GROUP * gidx
    el = jnp.where(jnp.logical_and(lane >= lo, lane < lo + EXPERTS_PER_GROUP), logits, ninf)
    v1 = jnp.max(el, axis=-1, keepdims=True)
    i1 = jnp.min(jnp.where(el == v1, lane, LANES), axis=-1, keepdims=True)
    el2 = jnp.where(lane == i1, ninf, el)
    v2 = jnp.max(el2, axis=-1, keepdims=True)
    i2 = jnp.min(jnp.where(el2 == v2, lane, LANES), axis=-1, keepdims=True)
    e2 = jnp.exp(v2 - v1)
    den = 1.0 + e2
    return jnp.where(lane == i1, (1.0 / den) * g_val, 0.0) + jnp.where(lane == i2, (e2 / den) * g_val, 0.0)


def _merge_kernel(x_ref, sb_ref, lat_ref, ln1_ref, wg_ref, wuv_ref, wsb_ref, wmla_ref, wout_ref, ln2_ref, wrh_ref, wrl_ref,
                  br_ref, h_ref, xf_ref, comb_ref, *, d):
    x = x_ref[...]
    xn = _rms(x, ln1_ref[...]).astype(BF16)
    g = _dot(xn, wg_ref[...])
    mla_o = _dot(lat_ref[...], wuv_ref[...]).astype(BF16)
    merged = jax.nn.sigmoid(g[:, :d]) * _dot(sb_ref[...], wsb_ref[...]) + jax.nn.sigmoid(g[:, d:]) * _dot(mla_o, wmla_ref[...])
    h = x + _dot(merged.astype(BF16), wout_ref[...])
    h_ref[...] = h
    xf = _rms(h, ln2_ref[...])
    xf_hi = xf.astype(BF16)
    xf_ref[...] = xf_hi
    xf_lo = (xf - xf_hi.astype(F32)).astype(BF16)
    wrh = wrh_ref[...]
    logits = _dot(xf_hi, wrh) + _dot(xf_lo, wrh) + _dot(xf_hi, wrl_ref[...]) + br_ref[...]
    comb_ref[...] = _route(logits)


def _merge(x, sb_o, lat, ln1, wg, wuv, wsb, wmla, wout, ln2, wrh, wrl, br, *, tm):
    N, D = x.shape
    tok = lambda w: pl.BlockSpec((tm, w), lambda i: (i, 0))
    consts = (ln1, wg, wuv, wsb, wmla, wout, ln2, wrh, wrl, br)
    return pl.pallas_call(
        functools.partial(_merge_kernel, d=D),
        out_shape=(jax.ShapeDtypeStruct((N, D), F32), jax.ShapeDtypeStruct((N, D), BF16), jax.ShapeDtypeStruct((N, LANES), F32)),
        grid=(N // tm,), in_specs=[tok(D), tok(sb_o.shape[1]), tok(lat.shape[1])] + [_const_spec(a.shape) for a in consts],
        out_specs=(tok(D), tok(D), tok(LANES)), compiler_params=_params(("parallel",)), name="merge")(
        x, sb_o, lat, *consts)


def _moe_kernel(xf_ref, comb_ref, h_ref, wg_ref, wu_ref, wd_ref, lnf_ref, y_ref, acc_ref, *, final_norm):
    e = pl.program_id(1)

    @pl.when(e == 0)
    def _():
        acc_ref[...] = jnp.zeros(acc_ref.shape, F32)

    xf = xf_ref[...]
    comb = comb_ref[...]
    lane = lax.broadcasted_iota(jnp.int32, comb.shape, 1)
    c = jnp.sum(jnp.where(lane == e + ROUTER_LANE0, comb, 0.0), axis=-1, keepdims=True)
    act = jax.nn.silu(_dot(xf, wg_ref[0])) * _dot(xf, wu_ref[0])
    acc_ref[...] += _dot((act * c).astype(BF16), wd_ref[0])

    @pl.when(e == pl.num_programs(1) - 1)
    def _():
        y = h_ref[...] + acc_ref[...]
        y_ref[...] = _rms(y, lnf_ref[...]) if final_norm else y


def _moe(xf, comb, h, wg, wu, wd, lnf, *, tm, final_norm):
    N, D = h.shape
    E, _, F = wg.shape
    tok = lambda w: pl.BlockSpec((tm, w), lambda i, e: (i, 0))
    return pl.pallas_call(
        functools.partial(_moe_kernel, final_norm=final_norm), out_shape=jax.ShapeDtypeStruct((N, D), F32), grid=(N // tm, E),
        in_specs=[tok(D), tok(LANES), tok(D), pl.BlockSpec((1, D, F), lambda i, e: (e, 0, 0)),
                  pl.BlockSpec((1, D, F), lambda i, e: (e, 0, 0)), pl.BlockSpec((1, F, D), lambda i, e: (e, 0, 0)),
                  pl.BlockSpec((1, D), lambda i, e: (0, 0))],
        out_specs=tok(D), scratch_shapes=[pltpu.VMEM((tm, D), F32)],
        compiler_params=_params(("parallel", "arbitrary")), name="moe")(xf, comb, h, wg, wu, wd, lnf)


def _pad_lanes(w):
    return jnp.pad(w, [(0, 0)] * (w.ndim - 1) + [(0, LANES - w.shape[-1])])


def _swap_halves(w):
    half = w.shape[-1] // 2
    return jnp.concatenate([w[..., half:], w[..., :half]], axis=-1)


def _rope_tables(pos, rope):
    half = rope // 2
    inv_freq = ROPE_BASE ** (-jnp.arange(half, dtype=F32) / half)
    ang = pos.astype(F32)[:, None] * inv_freq[None, :]
    cos, sin = jnp.cos(ang), jnp.sin(ang)
    return _pad_lanes(jnp.concatenate([cos, cos], axis=-1)), _pad_lanes(jnp.concatenate([-sin, sin], axis=-1))


def _sb_scan_operator():
    n = SB_KBLK
    j = jnp.arange(n)[:, None]
    s = jnp.arange(n)[None, :]
    tri = (j > s).astype(F32)
    one = jnp.ones((n, n), F32)
    z = jnp.zeros((n, n), F32)
    return jnp.concatenate([jnp.concatenate([tri, z, one, z], axis=1), jnp.concatenate([z, tri, z, one], axis=1)], axis=0).astype(BF16)


def _pad_rows(a, rows):
    return jnp.pad(a, [(0, 0), (0, rows - a.shape[1]), (0, 0)])


def kernel(x_prompt, x_sample, cache_sb_k, cache_sb_v, cache_mla_ckv, cache_mla_kpe, ln1, w_in, q_norm, w_q_up, kv_norm, w_uk,
           w_uv, w_sb_branch, w_mla_branch, w_out, ln2, w_group, b_group, w_router, b_router, w_gate, w_up, w_down, ln_f):
    B, T, D = x_prompt.shape
    SBt, SQ, _ = x_sample.shape
    depth = ln1.shape[0]
    past = cache_sb_k.shape[2]
    sbw = cache_sb_k.shape[3] * cache_sb_k.shape[4]
    sb_hd = cache_sb_k.shape[4]
    qlora = q_norm.shape[1]
    kvlora = kv_norm.shape[1]
    rope = cache_mla_kpe.shape[3]
    nope = w_uk.shape[3]
    vdim = w_uv.shape[3]
    H = w_uk.shape[2]
    sb_scale = 1.0 / math.sqrt(sb_hd)
    mla_scale = 1.0 / math.sqrt(nope + rope)

    cos_p, sin_p = _rope_tables(jnp.arange(T, dtype=jnp.int32), rope)
    pos_s = past + jnp.arange(SQ, dtype=jnp.int32)
    cos_s, sin_s = _rope_tables(jnp.tile(pos_s, SBt), rope)
    tri = _sb_scan_operator()
    eye = jnp.eye(H, dtype=F32)
    row = lambda a: a.reshape(1, -1)

    tm_p = min(512, T)
    tq_sb = min(256, T)
    tq_mla = min(256, T)
    ns = SBt * SQ
    tk_all = -(-(past + SQ) // SB_KBLK) * SB_KBLK

    xp = x_prompt
    xs = x_sample.reshape(1, ns, D)
    outs_p = [[], [], [], []]
    outs_s = [[], [], [], []]
    for l in range(depth):
        wl = w_in[l]
        c = 3 * sbw + qlora + kvlora
        wkpe = wl[:, c:c + rope]
        wmain = jnp.concatenate([wl[:, :c], _pad_lanes(wkpe), _pad_lanes(_swap_halves(wkpe))], axis=1).astype(BF16)
        wgates = wl[:, c + rope:].astype(BF16)
        wq = w_q_up[l]
        wqn = wq[:, :, :nope].reshape(qlora, H * nope).astype(BF16)
        wqpe = _pad_lanes(wq[:, :, nope:]).reshape(qlora, H * LANES).astype(BF16)
        wqpes = _pad_lanes(_swap_halves(wq[:, :, nope:])).reshape(qlora, H * LANES).astype(BF16)
        wuk_bd = jnp.einsum('hdc,hg->hdgc', jnp.transpose(w_uk[l], (1, 2, 0)), eye).reshape(H * nope, H * kvlora).astype(BF16)
        wuv_bd = jnp.einsum('hcd,hg->hcgd', jnp.transpose(w_uv[l], (1, 0, 2)), eye).reshape(H * kvlora, H * vdim).astype(BF16)
        wr = _pad_lanes(jnp.concatenate([w_group[l], w_router[l]], axis=1))
        wrh = wr.astype(BF16)
        wrl = (wr - wrh.astype(F32)).astype(BF16)
        br = _pad_lanes(jnp.concatenate([b_group[l], b_router[l]]).reshape(1, -1))
        wsb = w_sb_branch[l].astype(BF16)
        wmla = w_mla_branch[l].astype(BF16)
        wo = w_out[l].astype(BF16)
        wg_e = w_gate[l].astype(BF16)
        wu_e = w_up[l].astype(BF16)
        wd_e = w_down[l].astype(BF16)
        lnf = row(ln_f)
        last = l == depth - 1
        proj_args = (row(ln1[l]), wmain, row(q_norm[l]), row(kv_norm[l]), wqn, wuk_bd, wqpe, wqpes)
        proj_kw = dict(sb_scale=sb_scale, mla_scale=mla_scale, sbw=sbw, qlora=qlora, kvlora=kvlora, rope=rope)
        merge_args = (row(ln1[l]), wgates, wuv_bd, wsb, wmla, wo, row(ln2[l]), wrh, wrl, br)

        q, k, v, kb, vb, ckv, kpe, kcat, qcat = _proj(xp, cos_p, sin_p, *proj_args, tm=tm_p, **proj_kw)
        sb_o = _sb_prompt(q, kb, vb, tri, tq=tq_sb)
        lat = _mla_prompt(qcat, kcat, tq=tq_mla, tk=tq_mla, kvlora=kvlora)
        h, xf, comb = _merge(xp.reshape(B * T, D), sb_o.reshape(B * T, sbw), lat.reshape(B * T, H * kvlora), *merge_args,
                             tm=tm_p)
        y = _moe(xf, comb, h, wg_e, wu_e, wd_e, lnf, tm=min(1024, B * T), final_norm=last)
        xp = y.reshape(B, T, D)
        for lst, a in zip(outs_p, (k, v, ckv, kpe)):
            lst.append(a)

        q2, k2, v2, kb2, vb2, ckv2, kpe2, kcat2, qcat2 = _proj(xs, cos_s, sin_s, *proj_args, tm=ns, **proj_kw)
        per_stream = lambda a: a.reshape(SBt, SQ, a.shape[-1])
        k_all = _pad_rows(jnp.concatenate([cache_sb_k[l].reshape(SBt, past, sbw).astype(BF16), per_stream(kb2)], axis=1), tk_all)
        v_all = _pad_rows(jnp.concatenate([cache_sb_v[l].reshape(SBt, past, sbw).astype(BF16), per_stream(vb2)], axis=1), tk_all)
        kc_past = jnp.concatenate([cache_mla_ckv[l], _pad_lanes(cache_mla_kpe[l])], axis=-1).astype(BF16)
        kcat_all = _pad_rows(jnp.concatenate([kc_past, per_stream(kcat2)], axis=1), tk_all)
        sb_o2 = _sb_sample(per_stream(q2), k_all, v_all, tri, qpos0=past)
        qcat_s = jnp.transpose(qcat2.reshape(H, SBt, SQ, kvlora + LANES), (1, 0, 2, 3))
        lat2 = _mla_sample(qcat_s, kcat_all, qpos0=past, nvalid=past + SQ, kvlora=kvlora)
        h2, xf2, comb2 = _merge(xs.reshape(ns, D), sb_o2.reshape(ns, sbw), lat2.reshape(ns, H * kvlora), *merge_args, tm=ns)
        y2 = _moe(xf2, comb2, h2, wg_e, wu_e, wd_e, lnf, tm=ns, final_norm=last)
        xs = y2.reshape(1, ns, D)
        for lst, a in zip(outs_s, (k2, v2, ckv2, kpe2)):
            lst.append(a)

    heads = lambda a, n: a.reshape(n, -1, SB_HEADS, sb_hd)
    return (xp, xs.reshape(SBt, SQ, D),
            jnp.stack([heads(a, B) for a in outs_p[0]]), jnp.stack([heads(a, B) for a in outs_p[1]]),
            jnp.stack(outs_p[2]), jnp.stack(outs_p[3]),
            jnp.stack([heads(a.reshape(SBt, SQ, sbw), SBt) for a in outs_s[0]]),
            jnp.stack([heads(a.reshape(SBt, SQ, sbw), SBt) for a in outs_s[1]]),
            jnp.stack([a.reshape(SBt, SQ, kvlora) for a in outs_s[2]]),
            jnp.stack([a.reshape(SBt, SQ, rope) for a in outs_s[3]]))
```

```python
import functools
import math

import jax
import jax.numpy as jnp
from jax import lax
from jax.experimental import pallas as pl
from jax.experimental.pallas import tpu as pltpu

F32 = jnp.float32
BF16 = jnp.bfloat16

CHUNK = 64
CHUNK_SHIFT = CHUNK.bit_length() - 1
assert 1 << CHUNK_SHIFT == CHUNK
SB_HEADS = 8
MLA_HEADS = 8
ROPE_BASE = 10000.0
N_GROUPS = 4
EXPERTS_PER_GROUP = 8
N_EXPERTS = N_GROUPS * EXPERTS_PER_GROUP
RMS_EPS = 1e-6

LANES = 128
SB_KBLK = LANES
SB_DEAD_CARRY = -110.0
MLA_ROW_GROUPS = 2
MLA_ROW_CHUNK = 64
LOG2E = math.log2(math.e)
VMEM_LIMIT = 56 * 1024 * 1024
ROUTER_LANE0 = N_GROUPS


def _dot(a, b):
    return jnp.dot(a, b, preferred_element_type=F32)


def _dot_nt(a, b):
    return lax.dot_general(a, b, (((1,), (1,)), ((), ())), preferred_element_type=F32)


def _rms(x, g):
    return x * lax.rsqrt(jnp.mean(x * x, axis=-1, keepdims=True) + RMS_EPS) * g


def _params(sem):
    return pltpu.CompilerParams(dimension_semantics=sem, vmem_limit_bytes=VMEM_LIMIT)


def _const_spec(shape):
    return pl.BlockSpec(shape, lambda *_: (0,) * len(shape))


def _proj_kernel(x_ref, cos_ref, sin_ref, ln1_ref, wmain_ref, qn_ref, kvn_ref, wqn_ref, wuk_ref, wqpe_ref, wqpes_ref,
                 q_ref, k_ref, v_ref, kb_ref, vb_ref, ckv_ref, kpe_ref, kcat_ref, vext_ref, qcat_ref, *, sb_scale, mla_scale, sbw,
                 qlora, kvlora, rope):
    xn = _rms(x_ref[0], ln1_ref[...]).astype(BF16)
    p = _dot(xn, wmain_ref[...])
    o = 0
    q = p[:, o:o + sbw]; o += sbw
    k = p[:, o:o + sbw]; o += sbw
    v = p[:, o:o + sbw]; o += sbw
    cq = p[:, o:o + qlora]; o += qlora
    ckv = p[:, o:o + kvlora]; o += kvlora
    kpe = p[:, o:o + LANES]; o += LANES
    kpe_sw = p[:, o:o + LANES]
    q_ref[0] = (q * sb_scale).astype(BF16)
    k_ref[0] = k
    v_ref[0] = v
    kb_ref[0] = k.astype(BF16)
    vb_ref[0] = v.astype(BF16)
    cos = cos_ref[...]
    sin = sin_ref[...]
    ckvn = _rms(ckv, kvn_ref[...])
    ckv_ref[0] = ckvn
    kpe_rot = kpe * cos + kpe_sw * sin
    kpe_ref[0] = kpe_rot[:, :rope]
    kcat_ref[0] = jnp.concatenate([ckvn, kpe_rot], axis=-1).astype(BF16)
    vext_ref[0] = jnp.concatenate([ckvn, jnp.ones_like(ckvn)], axis=-1).astype(BF16)
    cqn = _rms(cq, qn_ref[...]).astype(BF16)
    qnope = _dot(cqn, wqn_ref[...]).astype(BF16)
    qlat = _dot(qnope, wuk_ref[...])
    qpe = _dot(cqn, wqpe_ref[...])
    qpe_sw = _dot(cqn, wqpes_ref[...])
    for h in range(MLA_HEADS):
        sl = slice(h * LANES, (h + 1) * LANES)
        qcat_ref[0, h, :, :kvlora] = (qlat[:, h * kvlora:(h + 1) * kvlora] * mla_scale).astype(BF16)
        qcat_ref[0, h, :, kvlora:] = ((qpe[:, sl] * cos + qpe_sw[:, sl] * sin) * mla_scale).astype(BF16)


def _proj(x, cos_t, sin_t, ln1, wmain, qn, kvn, wqn, wuk, wqpe, wqpes, *, tm, sb_scale, mla_scale, sbw, qlora, kvlora, rope):
    B, T, D = x.shape
    grid = (B, T // tm)
    tok = lambda w: pl.BlockSpec((1, tm, w), lambda b, i: (b, i, 0))
    tab = pl.BlockSpec((tm, LANES), lambda b, i: (i, 0))
    kcw = kvlora + LANES
    out_shape = (
        jax.ShapeDtypeStruct((B, T, sbw), BF16),
        jax.ShapeDtypeStruct((B, T, sbw), F32),
        jax.ShapeDtypeStruct((B, T, sbw), F32),
        jax.ShapeDtypeStruct((B, T, sbw), BF16),
        jax.ShapeDtypeStruct((B, T, sbw), BF16),
        jax.ShapeDtypeStruct((B, T, kvlora), F32),
        jax.ShapeDtypeStruct((B, T, rope), F32),
        jax.ShapeDtypeStruct((B, T, kcw), BF16),
        jax.ShapeDtypeStruct((B, T, 2 * kvlora), BF16),
        jax.ShapeDtypeStruct((B, MLA_HEADS, T, kcw), BF16),
    )
    out_specs = (tok(sbw), tok(sbw), tok(sbw), tok(sbw), tok(sbw), tok(kvlora), tok(rope), tok(kcw), tok(2 * kvlora),
                 pl.BlockSpec((1, MLA_HEADS, tm, kcw), lambda b, i: (b, 0, i, 0)))
    in_specs = [tok(D), tab, tab] + [_const_spec(a.shape) for a in (ln1, wmain, qn, kvn, wqn, wuk, wqpe, wqpes)]
    kern = functools.partial(_proj_kernel, sb_scale=sb_scale, mla_scale=mla_scale, sbw=sbw, qlora=qlora, kvlora=kvlora,
                             rope=rope)
    return pl.pallas_call(kern, out_shape=out_shape, grid=grid, in_specs=in_specs, out_specs=out_specs,
                          compiler_params=_params(("parallel", "parallel")), name="proj")(
        x, cos_t, sin_t, ln1, wmain, qn, kvn, wqn, wuk, wqpe, wqpes)


def _head_pair_blockdiag(blk):
    lane = lax.broadcasted_iota(jnp.int32, blk.shape, 1)
    zero = jnp.zeros_like(blk)
    half = LANES // 2
    return jnp.concatenate([jnp.where(lane < half, blk, zero), jnp.where(lane >= half, blk, zero)], axis=0)


def _sb_step(q, kblk, vblk, tri, carry, acc, qpos0, kpos0, masked):
    tq = q.shape[0]
    z = _dot_nt(q, _head_pair_blockdiag(kblk))
    lb = jnp.minimum(z, 0.0) - jnp.log1p(jnp.exp(-jnp.abs(z)))
    lom = lb - z
    if masked:
        row = lax.broadcasted_iota(jnp.int32, (tq, 2 * SB_KBLK), 0)
        col = lax.broadcasted_iota(jnp.int32, (tq, 2 * SB_KBLK), 1)
        keep = (kpos0 + (col & (SB_KBLK - 1))) < (qpos0 + row)
        lom = jnp.where(keep, lom, 0.0)
    hi = lom.astype(BF16)
    lo = (lom - hi.astype(F32)).astype(BF16)
    r = _dot(hi, tri) + _dot(lo, tri)
    suffix = r[:, :2 * SB_KBLK] + carry
    w = jnp.exp(lb + suffix)
    if masked:
        w = jnp.where(keep, w, 0.0)
    acc = acc + _dot(w.astype(BF16), _head_pair_blockdiag(vblk))
    carry = carry + r[:, 2 * SB_KBLK:]
    return carry, acc


def _sb_prompt_kernel(q_ref, k_ref, v_ref, tri_ref, o_ref, carry_ref, acc_ref, *, tq):
    i = pl.program_id(2)
    q = q_ref[0]
    tri = tri_ref[...]
    nk = tq // SB_KBLK
    qpos0 = i * tq
    carry = jnp.zeros((tq, 2 * SB_KBLK), F32)
    acc = jnp.zeros((tq, LANES), F32)
    for d in range(nk):
        j = i * nk + (nk - 1 - d)
        ks = pl.multiple_of(j * SB_KBLK, SB_KBLK)
        carry, acc = _sb_step(q, k_ref[0, pl.ds(ks, SB_KBLK), :], v_ref[0, pl.ds(ks, SB_KBLK), :], tri, carry, acc,
                              qpos0, j * SB_KBLK, True)
    carry_ref[...] = carry
    acc_ref[...] = acc

    def cond(s):
        j, live = s
        return jnp.logical_and(j >= 0, live > SB_DEAD_CARRY)

    def body(s):
        j, _ = s
        ks = pl.multiple_of(j * SB_KBLK, SB_KBLK)
        c, a = _sb_step(q, k_ref[0, pl.ds(ks, SB_KBLK), :], v_ref[0, pl.ds(ks, SB_KBLK), :], tri, carry_ref[...],
                        acc_ref[...], qpos0, 0, False)
        carry_ref[...] = c
        acc_ref[...] = a
        return j - 1, jnp.max(c)

    lax.while_loop(cond, body, (i * nk - 1, jnp.max(carry)))
    o_ref[0] = acc_ref[...].astype(o_ref.dtype)


def _sb_prompt(q, kb, vb, tri, *, tq):
    B, T, W = q.shape
    grid = (B, W // LANES, T // tq)
    qspec = pl.BlockSpec((1, tq, LANES), lambda b, p, i: (b, i, p))
    kspec = pl.BlockSpec((1, T, LANES), lambda b, p, i: (b, 0, p))
    return pl.pallas_call(
        functools.partial(_sb_prompt_kernel, tq=tq), out_shape=jax.ShapeDtypeStruct((B, T, W), BF16), grid=grid,
        in_specs=[qspec, kspec, kspec, _const_spec(tri.shape)], out_specs=qspec,
        scratch_shapes=[pltpu.VMEM((tq, 2 * SB_KBLK), F32), pltpu.VMEM((tq, LANES), F32)],
        compiler_params=_params(("parallel", "parallel", "arbitrary")), name="sb_prompt")(q, kb, vb, tri)


def _sb_sample_kernel(q_ref, k_ref, v_ref, tri_ref, o_ref, *, tq, nkb, qpos0):
    q = q_ref[0]
    tri = tri_ref[...]
    carry = jnp.zeros((tq, 2 * SB_KBLK), F32)
    acc = jnp.zeros((tq, LANES), F32)
    for j in range(nkb - 1, -1, -1):
        ks = j * SB_KBLK
        carry, acc = _sb_step(q, k_ref[0, ks:ks + SB_KBLK, :], v_ref[0, ks:ks + SB_KBLK, :], tri, carry, acc, qpos0, ks, True)
    o_ref[0] = acc.astype(o_ref.dtype)


def _sb_sample(q, k_all, v_all, tri, *, qpos0):
    B, tq, W = q.shape
    Tk = k_all.shape[1]
    grid = (B, W // LANES)
    qspec = pl.BlockSpec((1, tq, LANES), lambda b, p: (b, 0, p))
    kspec = pl.BlockSpec((1, Tk, LANES), lambda b, p: (b, 0, p))
    return pl.pallas_call(
        functools.partial(_sb_sample_kernel, tq=tq, nkb=Tk // SB_KBLK, qpos0=qpos0),
        out_shape=jax.ShapeDtypeStruct((B, tq, W), BF16), grid=grid,
        in_specs=[qspec, kspec, kspec, _const_spec(tri.shape)], out_specs=qspec,
        compiler_params=_params(("parallel", "parallel")), name="sb_sample")(q, k_all, v_all, tri)


def _mla_prompt_kernel(q_ref, kt_ref, vx_ref, o_ref, s_ref, p_ref, m_ref, a_ref, acc_ref, *, tq, tk, kvlora):
    i = pl.program_id(1)
    H = q_ref.shape[1]
    kc = q_ref.shape[3]
    group_heads = H // MLA_ROW_GROUPS
    gr = group_heads * tq
    m_ref[...] = jnp.full(m_ref.shape, -jnp.inf, F32)
    acc_ref[...] = jnp.zeros(acc_ref.shape, F32)

    def step(j, masked):
        ks = pl.multiple_of(j * tk, tk)
        kt = kt_ref[0, :, pl.ds(ks, tk)]
        vx = vx_ref[0, pl.ds(ks, tk), :]
        for g in range(MLA_ROW_GROUPS):
            qg = q_ref[0, g * group_heads:(g + 1) * group_heads].reshape(gr, kc)
            s_ref[g * gr:(g + 1) * gr, :] = _dot(qg, kt)
        for g in range(MLA_ROW_GROUPS):
            for c in range(gr // MLA_ROW_CHUNK):
                r0 = g * gr + c * MLA_ROW_CHUNK
                rs = slice(r0, r0 + MLA_ROW_CHUNK)
                tiles = [s_ref[rs, t * LANES:(t + 1) * LANES] for t in range(tk // LANES)]
                if masked:
                    row = lax.broadcasted_iota(jnp.int32, (MLA_ROW_CHUNK, LANES), 0)
                    col = lax.broadcasted_iota(jnp.int32, (MLA_ROW_CHUNK, LANES), 1)
                    qchunk = (i * tq + (r0 % tq) + row) >> CHUNK_SHIFT
                    tiles = [jnp.where(((j * tk + t * LANES + col) >> CHUNK_SHIFT) <= qchunk, st, -jnp.inf)
                             for t, st in enumerate(tiles)]
                smax = functools.reduce(jnp.maximum, tiles)
                m_old = m_ref[rs, :]
                m_new = jnp.maximum(m_old, jnp.max(smax, axis=-1, keepdims=True))
                for t, st in enumerate(tiles):
                    p_ref[rs, t * LANES:(t + 1) * LANES] = jnp.exp2(st - m_new).astype(BF16)
                a_ref[rs, :] = jnp.exp2(m_old - m_new)
                m_ref[rs, :] = m_new
            gs = slice(g * gr, (g + 1) * gr)
            alpha = a_ref[gs, :]
            acc_ref[gs, :] = jnp.concatenate([alpha, alpha], axis=-1) * acc_ref[gs, :] + _dot(p_ref[gs, :], vx)

    nfull = (i * tq) // tk

    def body(j, c):
        step(j, False)
        return c

    lax.fori_loop(0, nfull, body, 0)
    for d in range(tq // tk):
        step(nfull + d, True)
    for h in range(H):
        hs = slice(h * tq, (h + 1) * tq)
        o_ref[0, :, h * kvlora:(h + 1) * kvlora] = (acc_ref[hs, :kvlora] / acc_ref[hs, kvlora:]).astype(o_ref.dtype)


def _mla_prompt(qcat, kcat_t, vext, *, tq, tk, kvlora):
    B, H, T, KC = qcat.shape
    grid = (B, T // tq)
    rows = H * tq
    return pl.pallas_call(
        functools.partial(_mla_prompt_kernel, tq=tq, tk=tk, kvlora=kvlora),
        out_shape=jax.ShapeDtypeStruct((B, T, H * kvlora), BF16), grid=grid,
        in_specs=[pl.BlockSpec((1, H, tq, KC), lambda b, i: (b, 0, i, 0)), pl.BlockSpec((1, KC, T), lambda b, i: (b, 0, 0)),
                  pl.BlockSpec((1, T, 2 * kvlora), lambda b, i: (b, 0, 0))],
        out_specs=pl.BlockSpec((1, tq, H * kvlora), lambda b, i: (b, i, 0)),
        scratch_shapes=[pltpu.VMEM((rows, tk), F32), pltpu.VMEM((rows, tk), BF16), pltpu.VMEM((rows, LANES), F32),
                        pltpu.VMEM((rows, LANES), F32), pltpu.VMEM((rows, 2 * kvlora), F32)],
        compiler_params=_params(("parallel", "arbitrary")), name="mla_prompt")(qcat, kcat_t, vext)


def _mla_sample_kernel(q_ref, kc_ref, o_ref, *, tq, qpos0, nvalid, kvlora):
    H = q_ref.shape[1]
    q = q_ref[0].reshape(H * tq, q_ref.shape[3])
    kall = kc_ref[0]
    s = _dot_nt(q, kall)
    row = lax.broadcasted_iota(jnp.int32, s.shape, 0)
    col = lax.broadcasted_iota(jnp.int32, s.shape, 1)
    qpos = qpos0 + (row & (tq - 1))
    keep = jnp.logical_and((col >> CHUNK_SHIFT) <= (qpos >> CHUNK_SHIFT), col < nvalid)
    s = jnp.where(keep, s, -jnp.inf)
    p = jnp.exp2(s - jnp.max(s, axis=-1, keepdims=True))
    out = _dot(p.astype(BF16), kall[:, :kvlora]) / jnp.sum(p, axis=-1, keepdims=True)
    for h in range(H):
        o_ref[0, :, h * kvlora:(h + 1) * kvlora] = out[h * tq:(h + 1) * tq].astype(o_ref.dtype)


def _mla_sample(qcat, kcat_all, *, qpos0, nvalid, kvlora):
    B, H, tq, KC = qcat.shape
    Tk = kcat_all.shape[1]
    return pl.pallas_call(
        functools.partial(_mla_sample_kernel, tq=tq, qpos0=qpos0, nvalid=nvalid, kvlora=kvlora),
        out_shape=jax.ShapeDtypeStruct((B, tq, H * kvlora), BF16), grid=(B,),
        in_specs=[pl.BlockSpec((1, H, tq, KC), lambda b: (b, 0, 0, 0)), pl.BlockSpec((1, Tk, KC), lambda b: (b, 0, 0))],
        out_specs=pl.BlockSpec((1, tq, H * kvlora), lambda b: (b, 0, 0)),
        compiler_params=_params(("parallel",)), name="mla_sample")(qcat, kcat_all)


def _route(logits):
    lane = lax.broadcasted_iota(jnp.int32, logits.shape, 1)
    ninf = -jnp.inf
    gl = jnp.where(lane < N_GROUPS, logits, ninf)
    gmax = jnp.max(gl, axis=-1, keepdims=True)
    gidx = jnp.min(jnp.where(gl == gmax, lane, LANES), axis=-1, keepdims=True)
    g_val = 1.0 / jnp.sum(jnp.where(lane < N_GROUPS, jnp.exp(logits - gmax), 0.0), axis=-1, keepdims=True)
    lo = ROUTER_LANE0 + EXPERTS_PER_GROUP * gidx
    el = jnp.where(jnp.logical_and(lane >= lo, lane < lo + EXPERTS_PER_GROUP), logits, ninf)
    v1 = jnp.max(el, axis=-1, keepdims=True)
    i1 = jnp.min(jnp.where(el == v1, lane, LANES), axis=-1, keepdims=True)
    el2 = jnp.where(lane == i1, ninf, el)
    v2 = jnp.max(el2, axis=-1, keepdims=True)
    i2 = jnp.min(jnp.where(el2 == v2, lane, LANES), axis=-1, keepdims=True)
    e2 = jnp.exp(v2 - v1)
    den = 1.0 + e2
    return jnp.where(lane == i1, (1.0 / den) * g_val, 0.0) + jnp.where(lane == i2, (e2 / den) * g_val, 0.0)


def _merge_kernel(x_ref, sb_ref, lat_ref, ln1_ref, wg_ref, wuv_ref, wsb_ref, wmla_ref, wout_ref, ln2_ref, wrh_ref, wrl_ref,
                  br_ref, h_ref, xf_ref, comb_ref, *, d):
    x = x_ref[...]
    xn = _rms(x, ln1_ref[...]).astype(BF16)
    g = _dot(xn, wg_ref[...])
    mla_o = _dot(lat_ref[...], wuv_ref[...]).astype(BF16)
    merged = jax.nn.sigmoid(g[:, :d]) * _dot(sb_ref[...], wsb_ref[...]) + jax.nn.sigmoid(g[:, d:]) * _dot(mla_o, wmla_ref[...])
    h = x + _dot(merged.astype(BF16), wout_ref[...])
    h_ref[...] = h
    xf = _rms(h, ln2_ref[...])
    xf_hi = xf.astype(BF16)
    xf_ref[...] = xf_hi
    xf_lo = (xf - xf_hi.astype(F32)).astype(BF16)
    wrh = wrh_ref[...]
    logits = _dot(xf_hi, wrh) + _dot(xf_lo, wrh) + _dot(xf_hi, wrl_ref[...]) + br_ref[...]
    comb_ref[...] = _route(logits)


def _merge(x, sb_o, lat, ln1, wg, wuv, wsb, wmla, wout, ln2, wrh, wrl, br, *, tm):
    N, D = x.shape
    tok = lambda w: pl.BlockSpec((tm, w), lambda i: (i, 0))
    consts = (ln1, wg, wuv, wsb, wmla, wout, ln2, wrh, wrl, br)
    return pl.pallas_call(
        functools.partial(_merge_kernel, d=D),
        out_shape=(jax.ShapeDtypeStruct((N, D), F32), jax.ShapeDtypeStruct((N, D), BF16), jax.ShapeDtypeStruct((N, LANES), F32)),
        grid=(N // tm,), in_specs=[tok(D), tok(sb_o.shape[1]), tok(lat.shape[1])] + [_const_spec(a.shape) for a in consts],
        out_specs=(tok(D), tok(D), tok(LANES)), compiler_params=_params(("parallel",)), name="merge")(
        x, sb_o, lat, *consts)


def _moe_kernel(xf_ref, comb_ref, h_ref, wg_ref, wu_ref, wd_ref, lnf_ref, y_ref, acc_ref, *, final_norm):
    e = pl.program_id(1)

    @pl.when(e == 0)
    def _():
        acc_ref[...] = jnp.zeros(acc_ref.shape, F32)

    xf = xf_ref[...]
    comb = comb_ref[...]
    lane = lax.broadcasted_iota(jnp.int32, comb.shape, 1)
    c = jnp.sum(jnp.where(lane == e + ROUTER_LANE0, comb, 0.0), axis=-1, keepdims=True)
    act = jax.nn.silu(_dot(xf, wg_ref[0])) * _dot(xf, wu_ref[0])
    acc_ref[...] += _dot((act * c).astype(BF16), wd_ref[0])

    @pl.when(e == pl.num_programs(1) - 1)
    def _():
        y = h_ref[...] + acc_ref[...]
        y_ref[...] = _rms(y, lnf_ref[...]) if final_norm else y


def _moe(xf, comb, h, wg, wu, wd, lnf, *, tm, final_norm):
    N, D = h.shape
    E, _, F = wg.shape
    tok = lambda w: pl.BlockSpec((tm, w), lambda i, e: (i, 0))
    return pl.pallas_call(
        functools.partial(_moe_kernel, final_norm=final_norm), out_shape=jax.ShapeDtypeStruct((N, D), F32), grid=(N // tm, E),
        in_specs=[tok(D), tok(LANES), tok(D), pl.BlockSpec((1, D, F), lambda i, e: (e, 0, 0)),
                  pl.BlockSpec((1, D, F), lambda i, e: (e, 0, 0)), pl.BlockSpec((1, F, D), lambda i, e: (e, 0, 0)),
                  pl.BlockSpec((1, D), lambda i, e: (0, 0))],
        out_specs=tok(D), scratch_shapes=[pltpu.VMEM((tm, D), F32)],
        compiler_params=_params(("parallel", "arbitrary")), name="moe")(xf, comb, h, wg, wu, wd, lnf)


def _pad_lanes(w):
    return jnp.pad(w, [(0, 0)] * (w.ndim - 1) + [(0, LANES - w.shape[-1])])


def _swap_halves(w):
    half = w.shape[-1] // 2
    return jnp.concatenate([w[..., half:], w[..., :half]], axis=-1)


def _rope_tables(pos, rope):
    half = rope // 2
    inv_freq = ROPE_BASE ** (-jnp.arange(half, dtype=F32) / half)
    ang = pos.astype(F32)[:, None] * inv_freq[None, :]
    cos, sin = jnp.cos(ang), jnp.sin(ang)
    return _pad_lanes(jnp.concatenate([cos, cos], axis=-1)), _pad_lanes(jnp.concatenate([-sin, sin], axis=-1))


def _sb_scan_operator():
    n = SB_KBLK
    j = jnp.arange(n)[:, None]
    s = jnp.arange(n)[None, :]
    tri = (j > s).astype(F32)
    one = jnp.ones((n, n), F32)
    z = jnp.zeros((n, n), F32)
    return jnp.concatenate([jnp.concatenate([tri, z, one, z], axis=1), jnp.concatenate([z, tri, z, one], axis=1)], axis=0).astype(BF16)


def _pad_rows(a, rows):
    return jnp.pad(a, [(0, 0), (0, rows - a.shape[1]), (0, 0)])


def kernel(x_prompt, x_sample, cache_sb_k, cache_sb_v, cache_mla_ckv, cache_mla_kpe, ln1, w_in, q_norm, w_q_up, kv_norm, w_uk,
           w_uv, w_sb_branch, w_mla_branch, w_out, ln2, w_group, b_group, w_router, b_router, w_gate, w_up, w_down, ln_f):
    B, T, D = x_prompt.shape
    SBt, SQ, _ = x_sample.shape
    depth = ln1.shape[0]
    past = cache_sb_k.shape[2]
    sbw = cache_sb_k.shape[3] * cache_sb_k.shape[4]
    sb_hd = cache_sb_k.shape[4]
    qlora = q_norm.shape[1]
    kvlora = kv_norm.shape[1]
    rope = cache_mla_kpe.shape[3]
    nope = w_uk.shape[3]
    vdim = w_uv.shape[3]
    H = w_uk.shape[2]
    sb_scale = 1.0 / math.sqrt(sb_hd)
    mla_scale = LOG2E / math.sqrt(nope + rope)

    cos_p, sin_p = _rope_tables(jnp.arange(T, dtype=jnp.int32), rope)
    pos_s = past + jnp.arange(SQ, dtype=jnp.int32)
    cos_s, sin_s = _rope_tables(jnp.tile(pos_s, SBt), rope)
    tri = _sb_scan_operator()
    eye = jnp.eye(H, dtype=F32)
    row = lambda a: a.reshape(1, -1)

    tm_p = min(512, T)
    tq_sb = min(256, T)
    tq_mla = min(512, T)
    ns = SBt * SQ
    tk_all = -(-(past + SQ) // SB_KBLK) * SB_KBLK

    xp = x_prompt
    xs = x_sample.reshape(1, ns, D)
    outs_p = [[], [], [], []]
    outs_s = [[], [], [], []]
    for l in range(depth):
        wl = w_in[l]
        c = 3 * sbw + qlora + kvlora
        wkpe = wl[:, c:c + rope]
        wmain = jnp.concatenate([wl[:, :c], _pad_lanes(wkpe), _pad_lanes(_swap_halves(wkpe))], axis=1).astype(BF16)
        wgates = wl[:, c + rope:].astype(BF16)
        wq = w_q_up[l]
        wqn = wq[:, :, :nope].reshape(qlora, H * nope).astype(BF16)
        wqpe = _pad_lanes(wq[:, :, nope:]).reshape(qlora, H * LANES).astype(BF16)
        wqpes = _pad_lanes(_swap_halves(wq[:, :, nope:])).reshape(qlora, H * LANES).astype(BF16)
        wuk_bd = jnp.einsum('hdc,hg->hdgc', jnp.transpose(w_uk[l], (1, 2, 0)), eye).reshape(H * nope, H * kvlora).astype(BF16)
        wuv_bd = jnp.einsum('hcd,hg->hcgd', jnp.transpose(w_uv[l], (1, 0, 2)), eye).reshape(H * kvlora, H * vdim).astype(BF16)
        wr = _pad_lanes(jnp.concatenate([w_group[l], w_router[l]], axis=1))
        wrh = wr.astype(BF16)
        wrl = (wr - wrh.astype(F32)).astype(BF16)
        br = _pad_lanes(jnp.concatenate([b_group[l], b_router[l]]).reshape(1, -1))
        wsb = w_sb_branch[l].astype(BF16)
        wmla = w_mla_branch[l].astype(BF16)
        wo = w_out[l].astype(BF16)
        wg_e = w_gate[l].astype(BF16)
        wu_e = w_up[l].astype(BF16)
        wd_e = w_down[l].astype(BF16)
        lnf = row(ln_f)
        last = l == depth - 1
        proj_args = (row(ln1[l]), wmain, row(q_norm[l]), row(kv_norm[l]), wqn, wuk_bd, wqpe, wqpes)
        proj_kw = dict(sb_scale=sb_scale, mla_scale=mla_scale, sbw=sbw, qlora=qlora, kvlora=kvlora, rope=rope)
        merge_args = (row(ln1[l]), wgates, wuv_bd, wsb, wmla, wo, row(ln2[l]), wrh, wrl, br)

        q, k, v, kb, vb, ckv, kpe, kcat, vext, qcat = _proj(xp, cos_p, sin_p, *proj_args, tm=tm_p, **proj_kw)
        sb_o = _sb_prompt(q, kb, vb, tri, tq=tq_sb)
        lat = _mla_prompt(qcat, jnp.swapaxes(kcat, 1, 2), vext, tq=tq_mla, tk=tq_mla, kvlora=kvlora)
        h, xf, comb = _merge(xp.reshape(B * T, D), sb_o.reshape(B * T, sbw), lat.reshape(B * T, H * kvlora), *merge_args,
                             tm=tm_p)
        y = _moe(xf, comb, h, wg_e, wu_e, wd_e, lnf, tm=min(1024, B * T), final_norm=last)
        xp = y.reshape(B, T, D)
        for lst, a in zip(outs_p, (k, v, ckv, kpe)):
            lst.append(a)

        q2, k2, v2, kb2, vb2, ckv2, kpe2, kcat2, _, qcat2 = _proj(xs, cos_s, sin_s, *proj_args, tm=ns, **proj_kw)
        per_stream = lambda a: a.reshape(SBt, SQ, a.shape[-1])
        k_all = _pad_rows(jnp.concatenate([cache_sb_k[l].reshape(SBt, past, sbw).astype(BF16), per_stream(kb2)], axis=1), tk_all)
        v_all = _pad_rows(jnp.concatenate([cache_sb_v[l].reshape(SBt, past, sbw).astype(BF16), per_stream(vb2)], axis=1), tk_all)
        kc_past = jnp.concatenate([cache_mla_ckv[l], _pad_lanes(cache_mla_kpe[l])], axis=-1).astype(BF16)
        kcat_all = _pad_rows(jnp.concatenate([kc_past, per_stream(kcat2)], axis=1), tk_all)
        sb_o2 = _sb_sample(per_stream(q2), k_all, v_all, tri, qpos0=past)
        qcat_s = jnp.transpose(qcat2.reshape(H, SBt, SQ, kvlora + LANES), (1, 0, 2, 3))
        lat2 = _mla_sample(qcat_s, kcat_all, qpos0=past, nvalid=past + SQ, kvlora=kvlora)
        h2, xf2, comb2 = _merge(xs.reshape(ns, D), sb_o2.reshape(ns, sbw), lat2.reshape(ns, H * kvlora), *merge_args, tm=ns)
        y2 = _moe(xf2, comb2, h2, wg_e, wu_e, wd_e, lnf, tm=ns, final_norm=last)
        xs = y2.reshape(1, ns, D)
        for lst, a in zip(outs_s, (k2, v2, ckv2, kpe2)):
            lst.append(a)

    heads = lambda a, n: a.reshape(n, -1, SB_HEADS, sb_hd)
    return (xp, xs.reshape(SBt, SQ, D),
            jnp.stack([heads(a, B) for a in outs_p[0]]), jnp.stack([heads(a, B) for a in outs_p[1]]),
            jnp.stack(outs_p[2]), jnp.stack(outs_p[3]),
            jnp.stack([heads(a.reshape(SBt, SQ, sbw), SBt) for a in outs_s[0]]),
            jnp.stack([heads(a.reshape(SBt, SQ, sbw), SBt) for a in outs_s[1]]),
            jnp.stack([a.reshape(SBt, SQ, kvlora) for a in outs_s[2]]),
            jnp.stack([a.reshape(SBt, SQ, rope) for a in outs_s[3]]))
```

```python
import functools
import math

import jax
import jax.numpy as jnp
from jax import lax
from jax.experimental import pallas as pl
from jax.experimental.pallas import tpu as pltpu

F32 = jnp.float32
BF16 = jnp.bfloat16

CHUNK = 64
CHUNK_SHIFT = CHUNK.bit_length() - 1
assert 1 << CHUNK_SHIFT == CHUNK
SB_HEADS = 8
MLA_HEADS = 8
ROPE_BASE = 10000.0
N_GROUPS = 4
EXPERTS_PER_GROUP = 8
N_EXPERTS = N_GROUPS * EXPERTS_PER_GROUP
RMS_EPS = 1e-6

LANES = 128
SB_KBLK = LANES
SB_DEAD_CARRY = -110.0
MLA_ROW_GROUPS = 2
MLA_ROW_CHUNK = 64
LOG2E = math.log2(math.e)
VMEM_LIMIT = 56 * 1024 * 1024
ROUTER_LANE0 = N_GROUPS


def _dot(a, b):
    return jnp.dot(a, b, preferred_element_type=F32)


def _dot_nt(a, b):
    return lax.dot_general(a, b, (((1,), (1,)), ((), ())), preferred_element_type=F32)


def _rms(x, g):
    return x * lax.rsqrt(jnp.mean(x * x, axis=-1, keepdims=True) + RMS_EPS) * g


def _params(sem):
    return pltpu.CompilerParams(dimension_semantics=sem, vmem_limit_bytes=VMEM_LIMIT)


def _const_spec(shape):
    return pl.BlockSpec(shape, lambda *_: (0,) * len(shape))


def _proj_kernel(x_ref, cos_ref, sin_ref, ln1_ref, wmain_ref, qn_ref, kvn_ref, wqn_ref, wuk_ref, wqpe_ref, wqpes_ref,
                 q_ref, k_ref, v_ref, kb_ref, vb_ref, ckv_ref, kpe_ref, kcat_ref, vext_ref, qcat_ref, *, sb_scale, mla_scale, sbw,
                 qlora, kvlora, rope):
    xn = _rms(x_ref[0], ln1_ref[...]).astype(BF16)
    p = _dot(xn, wmain_ref[...])
    o = 0
    q = p[:, o:o + sbw]; o += sbw
    k = p[:, o:o + sbw]; o += sbw
    v = p[:, o:o + sbw]; o += sbw
    cq = p[:, o:o + qlora]; o += qlora
    ckv = p[:, o:o + kvlora]; o += kvlora
    kpe = p[:, o:o + LANES]; o += LANES
    kpe_sw = p[:, o:o + LANES]
    q_ref[0] = (q * sb_scale).astype(BF16)
    k_ref[0] = k
    v_ref[0] = v
    kb_ref[0] = k.astype(BF16)
    vb_ref[0] = v.astype(BF16)
    cos = cos_ref[...]
    sin = sin_ref[...]
    ckvn = _rms(ckv, kvn_ref[...])
    ckv_ref[0] = ckvn
    kpe_rot = kpe * cos + kpe_sw * sin
    kpe_ref[0] = kpe_rot[:, :rope]
    kcat_ref[0] = jnp.concatenate([ckvn, kpe_rot], axis=-1).astype(BF16)
    vext_ref[0] = jnp.concatenate([ckvn, jnp.ones_like(ckvn)], axis=-1).astype(BF16)
    cqn = _rms(cq, qn_ref[...]).astype(BF16)
    qnope = _dot(cqn, wqn_ref[...]).astype(BF16)
    qlat = _dot(qnope, wuk_ref[...])
    qpe = _dot(cqn, wqpe_ref[...])
    qpe_sw = _dot(cqn, wqpes_ref[...])
    for h in range(MLA_HEADS):
        sl = slice(h * LANES, (h + 1) * LANES)
        qcat_ref[0, h, :, :kvlora] = (qlat[:, h * kvlora:(h + 1) * kvlora] * mla_scale).astype(BF16)
        qcat_ref[0, h, :, kvlora:] = ((qpe[:, sl] * cos + qpe_sw[:, sl] * sin) * mla_scale).astype(BF16)


def _proj(x, cos_t, sin_t, ln1, wmain, qn, kvn, wqn, wuk, wqpe, wqpes, *, tm, sb_scale, mla_scale, sbw, qlora, kvlora, rope):
    B, T, D = x.shape
    grid = (B, T // tm)
    tok = lambda w: pl.BlockSpec((1, tm, w), lambda b, i: (b, i, 0))
    tab = pl.BlockSpec((tm, LANES), lambda b, i: (i, 0))
    kcw = kvlora + LANES
    out_shape = (
        jax.ShapeDtypeStruct((B, T, sbw), BF16),
        jax.ShapeDtypeStruct((B, T, sbw), F32),
        jax.ShapeDtypeStruct((B, T, sbw), F32),
        jax.ShapeDtypeStruct((B, T, sbw), BF16),
        jax.ShapeDtypeStruct((B, T, sbw), BF16),
        jax.ShapeDtypeStruct((B, T, kvlora), F32),
        jax.ShapeDtypeStruct((B, T, rope), F32),
        jax.ShapeDtypeStruct((B, T, kcw), BF16),
        jax.ShapeDtypeStruct((B, T, 2 * kvlora), BF16),
        jax.ShapeDtypeStruct((B, MLA_HEADS, T, kcw), BF16),
    )
    out_specs = (tok(sbw), tok(sbw), tok(sbw), tok(sbw), tok(sbw), tok(kvlora), tok(rope), tok(kcw), tok(2 * kvlora),
                 pl.BlockSpec((1, MLA_HEADS, tm, kcw), lambda b, i: (b, 0, i, 0)))
    in_specs = [tok(D), tab, tab] + [_const_spec(a.shape) for a in (ln1, wmain, qn, kvn, wqn, wuk, wqpe, wqpes)]
    kern = functools.partial(_proj_kernel, sb_scale=sb_scale, mla_scale=mla_scale, sbw=sbw, qlora=qlora, kvlora=kvlora,
                             rope=rope)
    return pl.pallas_call(kern, out_shape=out_shape, grid=grid, in_specs=in_specs, out_specs=out_specs,
                          compiler_params=_params(("parallel", "parallel")), name="proj")(
        x, cos_t, sin_t, ln1, wmain, qn, kvn, wqn, wuk, wqpe, wqpes)


def _head_pair_blockdiag(blk):
    lane = lax.broadcasted_iota(jnp.int32, blk.shape, 1)
    zero = jnp.zeros_like(blk)
    half = LANES // 2
    return jnp.concatenate([jnp.where(lane < half, blk, zero), jnp.where(lane >= half, blk, zero)], axis=0)


def _sb_step(qs, kblk, vblk, tri, carries, accs, keep):
    tq = qs[0].shape[0]
    lbs, loms, parts = [], [], []
    for p, q in enumerate(qs):
        z = _dot_nt(q, _head_pair_blockdiag(kblk[:, p * LANES:(p + 1) * LANES]))
        lb = jnp.minimum(z, 0.0) - jnp.log(1.0 + jnp.exp(-jnp.abs(z)))
        lom = lb - z
        if keep is not None:
            lom = jnp.where(keep, lom, 0.0)
        hi = lom.astype(BF16)
        lbs.append(lb)
        loms.append(lom)
        parts += [hi, (lom - hi.astype(F32)).astype(BF16)]
    r = _dot(jnp.concatenate(parts, axis=0), tri)
    new_c, new_a = [], []
    for p in range(len(qs)):
        incl = r[2 * p * tq:(2 * p + 1) * tq] + r[(2 * p + 1) * tq:(2 * p + 2) * tq]
        w = jnp.exp(lbs[p] + (incl - loms[p]) + carries[p])
        if keep is not None:
            w = jnp.where(keep, w, 0.0)
        new_a.append(accs[p] + _dot(w.astype(BF16), _head_pair_blockdiag(vblk[:, p * LANES:(p + 1) * LANES])))
        tot = jnp.concatenate([jnp.broadcast_to(incl[:, 0:1], (tq, SB_KBLK)),
                               jnp.broadcast_to(incl[:, SB_KBLK:SB_KBLK + 1], (tq, SB_KBLK))], axis=1)
        new_c.append(carries[p] + tot)
    return new_c, new_a


def _sb_prompt_kernel(q_ref, k_ref, v_ref, tri_ref, o_ref, carry_ref, acc_ref, *, tq):
    i = pl.program_id(1)
    npair = q_ref.shape[2] // LANES
    qs = [q_ref[0, :, p * LANES:(p + 1) * LANES] for p in range(npair)]
    tri = tri_ref[...]
    row = lax.broadcasted_iota(jnp.int32, (tq, 2 * SB_KBLK), 0)
    col = lax.broadcasted_iota(jnp.int32, (tq, 2 * SB_KBLK), 1)
    ks = pl.multiple_of(i * SB_KBLK, SB_KBLK)
    zeros_c = [jnp.zeros((tq, 2 * SB_KBLK), F32)] * npair
    zeros_a = [jnp.zeros((tq, LANES), F32)] * npair
    c, a = _sb_step(qs, k_ref[0, pl.ds(ks, SB_KBLK), :], v_ref[0, pl.ds(ks, SB_KBLK), :], tri, zeros_c, zeros_a,
                    (col & (SB_KBLK - 1)) < row)
    for p in range(npair):
        carry_ref[p] = c[p]
        acc_ref[p] = a[p]

    def cond(s):
        j, live = s
        return jnp.logical_and(j >= 0, live > SB_DEAD_CARRY)

    def body(s):
        j, _ = s
        ks = pl.multiple_of(j * SB_KBLK, SB_KBLK)
        c, a = _sb_step(qs, k_ref[0, pl.ds(ks, SB_KBLK), :], v_ref[0, pl.ds(ks, SB_KBLK), :], tri,
                        [carry_ref[p] for p in range(npair)], [acc_ref[p] for p in range(npair)], None)
        for p in range(npair):
            carry_ref[p] = c[p]
            acc_ref[p] = a[p]
        return j - 1, jnp.max(functools.reduce(jnp.maximum, c))

    lax.while_loop(cond, body, (i - 1, jnp.max(functools.reduce(jnp.maximum, c))))
    for p in range(npair):
        o_ref[0, :, p * LANES:(p + 1) * LANES] = acc_ref[p].astype(o_ref.dtype)


def _sb_prompt(q, kb, vb, tri):
    B, T, W = q.shape
    tq = SB_KBLK
    npair = W // LANES
    qspec = pl.BlockSpec((1, tq, W), lambda b, i: (b, i, 0))
    kspec = pl.BlockSpec((1, T, W), lambda b, i: (b, 0, 0))
    return pl.pallas_call(
        functools.partial(_sb_prompt_kernel, tq=tq), out_shape=jax.ShapeDtypeStruct((B, T, W), BF16), grid=(B, T // tq),
        in_specs=[qspec, kspec, kspec, _const_spec(tri.shape)], out_specs=qspec,
        scratch_shapes=[pltpu.VMEM((npair, tq, 2 * SB_KBLK), F32), pltpu.VMEM((npair, tq, LANES), F32)],
        compiler_params=_params(("parallel", "arbitrary")), name="sb_prompt")(q, kb, vb, tri)


def _sb_sample_kernel(q_ref, k_ref, v_ref, tri_ref, o_ref, *, tq, nkb, qpos0):
    npair = q_ref.shape[2] // LANES
    qs = [q_ref[0, :, p * LANES:(p + 1) * LANES] for p in range(npair)]
    tri = tri_ref[...]
    row = lax.broadcasted_iota(jnp.int32, (tq, 2 * SB_KBLK), 0)
    col = lax.broadcasted_iota(jnp.int32, (tq, 2 * SB_KBLK), 1)
    c = [jnp.zeros((tq, 2 * SB_KBLK), F32)] * npair
    a = [jnp.zeros((tq, LANES), F32)] * npair
    for j in range(nkb - 1, -1, -1):
        ks = j * SB_KBLK
        c, a = _sb_step(qs, k_ref[0, ks:ks + SB_KBLK, :], v_ref[0, ks:ks + SB_KBLK, :], tri, c, a,
                        (ks + (col & (SB_KBLK - 1))) < (qpos0 + row))
    for p in range(npair):
        o_ref[0, :, p * LANES:(p + 1) * LANES] = a[p].astype(o_ref.dtype)


def _sb_sample(q, k_all, v_all, tri, *, qpos0):
    B, tq, W = q.shape
    Tk = k_all.shape[1]
    qspec = pl.BlockSpec((1, tq, W), lambda b: (b, 0, 0))
    kspec = pl.BlockSpec((1, Tk, W), lambda b: (b, 0, 0))
    return pl.pallas_call(
        functools.partial(_sb_sample_kernel, tq=tq, nkb=Tk // SB_KBLK, qpos0=qpos0),
        out_shape=jax.ShapeDtypeStruct((B, tq, W), BF16), grid=(B,),
        in_specs=[qspec, kspec, kspec, _const_spec(tri.shape)], out_specs=qspec,
        compiler_params=_params(("parallel",)), name="sb_sample")(q, k_all, v_all, tri)


def _mla_prompt_kernel(q_ref, kt_ref, vx_ref, o_ref, s_ref, p_ref, m_ref, a_ref, acc_ref, *, tq, tk, kvlora):
    i = pl.program_id(1)
    H = q_ref.shape[1]
    kc = q_ref.shape[3]
    group_heads = H // MLA_ROW_GROUPS
    gr = group_heads * tq
    m_ref[...] = jnp.full(m_ref.shape, -jnp.inf, F32)
    acc_ref[...] = jnp.zeros(acc_ref.shape, F32)

    def step(j, masked):
        ks = pl.multiple_of(j * tk, tk)
        kt = kt_ref[0, :, pl.ds(ks, tk)]
        vx = vx_ref[0, pl.ds(ks, tk), :]
        for g in range(MLA_ROW_GROUPS):
            qg = q_ref[0, g * group_heads:(g + 1) * group_heads].reshape(gr, kc)
            s_ref[g * gr:(g + 1) * gr, :] = _dot(qg, kt)
        for g in range(MLA_ROW_GROUPS):
            for c in range(gr // MLA_ROW_CHUNK):
                r0 = g * gr + c * MLA_ROW_CHUNK
                rs = slice(r0, r0 + MLA_ROW_CHUNK)
                tiles = [s_ref[rs, t * LANES:(t + 1) * LANES] for t in range(tk // LANES)]
                if masked:
                    row = lax.broadcasted_iota(jnp.int32, (MLA_ROW_CHUNK, LANES), 0)
                    col = lax.broadcasted_iota(jnp.int32, (MLA_ROW_CHUNK, LANES), 1)
                    qchunk = (i * tq + (r0 % tq) + row) >> CHUNK_SHIFT
                    tiles = [jnp.where(((j * tk + t * LANES + col) >> CHUNK_SHIFT) <= qchunk, st, -jnp.inf)
                             for t, st in enumerate(tiles)]
                smax = functools.reduce(jnp.maximum, tiles)
                m_old = m_ref[rs, :]
                m_new = jnp.maximum(m_old, jnp.max(smax, axis=-1, keepdims=True))
                for t, st in enumerate(tiles):
                    p_ref[rs, t * LANES:(t + 1) * LANES] = jnp.exp2(st - m_new).astype(BF16)
                a_ref[rs, :] = jnp.exp2(m_old - m_new)
                m_ref[rs, :] = m_new
            gs = slice(g * gr, (g + 1) * gr)
            alpha = a_ref[gs, :]
            acc_ref[gs, :] = jnp.concatenate([alpha, alpha], axis=-1) * acc_ref[gs, :] + _dot(p_ref[gs, :], vx)

    nfull = (i * tq) // tk

    def body(j, c):
        step(j, False)
        return c

    lax.fori_loop(0, nfull, body, 0)
    for d in range(tq // tk):
        step(nfull + d, True)
    for h in range(H):
        hs = slice(h * tq, (h + 1) * tq)
        o_ref[0, :, h * kvlora:(h + 1) * kvlora] = (acc_ref[hs, :kvlora] / acc_ref[hs, kvlora:]).astype(o_ref.dtype)


def _mla_prompt(qcat, kcat_t, vext, *, tq, tk, kvlora):
    B, H, T, KC = qcat.shape
    grid = (B, T // tq)
    rows = H * tq
    return pl.pallas_call(
        functools.partial(_mla_prompt_kernel, tq=tq, tk=tk, kvlora=kvlora),
        out_shape=jax.ShapeDtypeStruct((B, T, H * kvlora), BF16), grid=grid,
        in_specs=[pl.BlockSpec((1, H, tq, KC), lambda b, i: (b, 0, i, 0)), pl.BlockSpec((1, KC, T), lambda b, i: (b, 0, 0)),
                  pl.BlockSpec((1, T, 2 * kvlora), lambda b, i: (b, 0, 0))],
        out_specs=pl.BlockSpec((1, tq, H * kvlora), lambda b, i: (b, i, 0)),
        scratch_shapes=[pltpu.VMEM((rows, tk), F32), pltpu.VMEM((rows, tk), BF16), pltpu.VMEM((rows, LANES), F32),
                        pltpu.VMEM((rows, LANES), F32), pltpu.VMEM((rows, 2 * kvlora), F32)],
        compiler_params=_params(("parallel", "arbitrary")), name="mla_prompt")(qcat, kcat_t, vext)


def _mla_sample_kernel(q_ref, kc_ref, o_ref, *, tq, qpos0, nvalid, kvlora):
    H = q_ref.shape[1]
    q = q_ref[0].reshape(H * tq, q_ref.shape[3])
    kall = kc_ref[0]
    s = _dot_nt(q, kall)
    row = lax.broadcasted_iota(jnp.int32, s.shape, 0)
    col = lax.broadcasted_iota(jnp.int32, s.shape, 1)
    qpos = qpos0 + (row & (tq - 1))
    keep = jnp.logical_and((col >> CHUNK_SHIFT) <= (qpos >> CHUNK_SHIFT), col < nvalid)
    s = jnp.where(keep, s, -jnp.inf)
    p = jnp.exp2(s - jnp.max(s, axis=-1, keepdims=True))
    out = _dot(p.astype(BF16), kall[:, :kvlora]) / jnp.sum(p, axis=-1, keepdims=True)
    for h in range(H):
        o_ref[0, :, h * kvlora:(h + 1) * kvlora] = out[h * tq:(h + 1) * tq].astype(o_ref.dtype)


def _mla_sample(qcat, kcat_all, *, qpos0, nvalid, kvlora):
    B, H, tq, KC = qcat.shape
    Tk = kcat_all.shape[1]
    return pl.pallas_call(
        functools.partial(_mla_sample_kernel, tq=tq, qpos0=qpos0, nvalid=nvalid, kvlora=kvlora),
        out_shape=jax.ShapeDtypeStruct((B, tq, H * kvlora), BF16), grid=(B,),
        in_specs=[pl.BlockSpec((1, H, tq, KC), lambda b: (b, 0, 0, 0)), pl.BlockSpec((1, Tk, KC), lambda b: (b, 0, 0))],
        out_specs=pl.BlockSpec((1, tq, H * kvlora), lambda b: (b, 0, 0)),
        compiler_params=_params(("parallel",)), name="mla_sample")(qcat, kcat_all)


def _route(logits):
    lane = lax.broadcasted_iota(jnp.int32, logits.shape, 1)
    ninf = -jnp.inf
    gl = jnp.where(lane < N_GROUPS, logits, ninf)
    gmax = jnp.max(gl, axis=-1, keepdims=True)
    gidx = jnp.min(jnp.where(gl == gmax, lane, LANES), axis=-1, keepdims=True)
    g_val = 1.0 / jnp.sum(jnp.where(lane < N_GROUPS, jnp.exp(logits - gmax), 0.0), axis=-1, keepdims=True)
    lo = ROUTER_LANE0 + EXPERTS_PER_GROUP * gidx
    el = jnp.where(jnp.logical_and(lane >= lo, lane < lo + EXPERTS_PER_GROUP), logits, ninf)
    v1 = jnp.max(el, axis=-1, keepdims=True)
    i1 = jnp.min(jnp.where(el == v1, lane, LANES), axis=-1, keepdims=True)
    el2 = jnp.where(lane == i1, ninf, el)
    v2 = jnp.max(el2, axis=-1, keepdims=True)
    i2 = jnp.min(jnp.where(el2 == v2, lane, LANES), axis=-1, keepdims=True)
    e2 = jnp.exp(v2 - v1)
    den = 1.0 + e2
    return jnp.where(lane == i1, (1.0 / den) * g_val, 0.0) + jnp.where(lane == i2, (e2 / den) * g_val, 0.0)


def _merge_kernel(x_ref, sb_ref, lat_ref, ln1_ref, wg_ref, wuv_ref, wsb_ref, wmla_ref, wout_ref, ln2_ref, wrh_ref, wrl_ref,
                  br_ref, h_ref, xf_ref, comb_ref, *, d):
    x = x_ref[...]
    xn = _rms(x, ln1_ref[...]).astype(BF16)
    g = _dot(xn, wg_ref[...])
    mla_o = _dot(lat_ref[...], wuv_ref[...]).astype(BF16)
    merged = jax.nn.sigmoid(g[:, :d]) * _dot(sb_ref[...], wsb_ref[...]) + jax.nn.sigmoid(g[:, d:]) * _dot(mla_o, wmla_ref[...])
    h = x + _dot(merged.astype(BF16), wout_ref[...])
    h_ref[...] = h
    xf = _rms(h, ln2_ref[...])
    xf_hi = xf.astype(BF16)
    xf_ref[...] = xf_hi
    xf_lo = (xf - xf_hi.astype(F32)).astype(BF16)
    wrh = wrh_ref[...]
    logits = _dot(xf_hi, wrh) + _dot(xf_lo, wrh) + _dot(xf_hi, wrl_ref[...]) + br_ref[...]
    comb_ref[...] = _route(logits)


def _merge(x, sb_o, lat, ln1, wg, wuv, wsb, wmla, wout, ln2, wrh, wrl, br, *, tm):
    N, D = x.shape
    tok = lambda w: pl.BlockSpec((tm, w), lambda i: (i, 0))
    consts = (ln1, wg, wuv, wsb, wmla, wout, ln2, wrh, wrl, br)
    return pl.pallas_call(
        functools.partial(_merge_kernel, d=D),
        out_shape=(jax.ShapeDtypeStruct((N, D), F32), jax.ShapeDtypeStruct((N, D), BF16), jax.ShapeDtypeStruct((N, LANES), F32)),
        grid=(N // tm,), in_specs=[tok(D), tok(sb_o.shape[1]), tok(lat.shape[1])] + [_const_spec(a.shape) for a in consts],
        out_specs=(tok(D), tok(D), tok(LANES)), compiler_params=_params(("parallel",)), name="merge")(
        x, sb_o, lat, *consts)


def _moe_kernel(xf_ref, comb_ref, h_ref, wg_ref, wu_ref, wd_ref, lnf_ref, y_ref, acc_ref, *, final_norm):
    e = pl.program_id(1)

    @pl.when(e == 0)
    def _():
        acc_ref[...] = jnp.zeros(acc_ref.shape, F32)

    xf = xf_ref[...]
    comb = comb_ref[...]
    lane = lax.broadcasted_iota(jnp.int32, comb.shape, 1)
    c = jnp.sum(jnp.where(lane == e + ROUTER_LANE0, comb, 0.0), axis=-1, keepdims=True)
    act = jax.nn.silu(_dot(xf, wg_ref[0])) * _dot(xf, wu_ref[0])
    acc_ref[...] += _dot((act * c).astype(BF16), wd_ref[0])

    @pl.when(e == pl.num_programs(1) - 1)
    def _():
        y = h_ref[...] + acc_ref[...]
        y_ref[...] = _rms(y, lnf_ref[...]) if final_norm else y


def _moe(xf, comb, h, wg, wu, wd, lnf, *, tm, final_norm):
    N, D = h.shape
    E, _, F = wg.shape
    tok = lambda w: pl.BlockSpec((tm, w), lambda i, e: (i, 0))
    return pl.pallas_call(
        functools.partial(_moe_kernel, final_norm=final_norm), out_shape=jax.ShapeDtypeStruct((N, D), F32), grid=(N // tm, E),
        in_specs=[tok(D), tok(LANES), tok(D), pl.BlockSpec((1, D, F), lambda i, e: (e, 0, 0)),
                  pl.BlockSpec((1, D, F), lambda i, e: (e, 0, 0)), pl.BlockSpec((1, F, D), lambda i, e: (e, 0, 0)),
                  pl.BlockSpec((1, D), lambda i, e: (0, 0))],
        out_specs=tok(D), scratch_shapes=[pltpu.VMEM((tm, D), F32)],
        compiler_params=_params(("parallel", "arbitrary")), name="moe")(xf, comb, h, wg, wu, wd, lnf)


def _pad_lanes(w):
    return jnp.pad(w, [(0, 0)] * (w.ndim - 1) + [(0, LANES - w.shape[-1])])


def _swap_halves(w):
    half = w.shape[-1] // 2
    return jnp.concatenate([w[..., half:], w[..., :half]], axis=-1)


def _rope_tables(pos, rope):
    half = rope // 2
    inv_freq = ROPE_BASE ** (-jnp.arange(half, dtype=F32) / half)
    ang = pos.astype(F32)[:, None] * inv_freq[None, :]
    cos, sin = jnp.cos(ang), jnp.sin(ang)
    return _pad_lanes(jnp.concatenate([cos, cos], axis=-1)), _pad_lanes(jnp.concatenate([-sin, sin], axis=-1))


def _sb_scan_operator():
    n = SB_KBLK
    tri = (jnp.arange(n)[:, None] >= jnp.arange(n)[None, :]).astype(F32)
    z = jnp.zeros((n, n), F32)
    return jnp.concatenate([jnp.concatenate([tri, z], axis=1), jnp.concatenate([z, tri], axis=1)], axis=0).astype(BF16)


def _pad_rows(a, rows):
    return jnp.pad(a, [(0, 0), (0, rows - a.shape[1]), (0, 0)])


def kernel(x_prompt, x_sample, cache_sb_k, cache_sb_v, cache_mla_ckv, cache_mla_kpe, ln1, w_in, q_norm, w_q_up, kv_norm, w_uk,
           w_uv, w_sb_branch, w_mla_branch, w_out, ln2, w_group, b_group, w_router, b_router, w_gate, w_up, w_down, ln_f):
    B, T, D = x_prompt.shape
    SBt, SQ, _ = x_sample.shape
    depth = ln1.shape[0]
    past = cache_sb_k.shape[2]
    sbw = cache_sb_k.shape[3] * cache_sb_k.shape[4]
    sb_hd = cache_sb_k.shape[4]
    qlora = q_norm.shape[1]
    kvlora = kv_norm.shape[1]
    rope = cache_mla_kpe.shape[3]
    nope = w_uk.shape[3]
    vdim = w_uv.shape[3]
    H = w_uk.shape[2]
    sb_scale = 1.0 / math.sqrt(sb_hd)
    mla_scale = LOG2E / math.sqrt(nope + rope)

    cos_p, sin_p = _rope_tables(jnp.arange(T, dtype=jnp.int32), rope)
    pos_s = past + jnp.arange(SQ, dtype=jnp.int32)
    cos_s, sin_s = _rope_tables(jnp.tile(pos_s, SBt), rope)
    tri = _sb_scan_operator()
    eye = jnp.eye(H, dtype=F32)
    row = lambda a: a.reshape(1, -1)

    tm_p = min(512, T)
    tq_mla = min(512, T)
    ns = SBt * SQ
    tk_all = -(-(past + SQ) // SB_KBLK) * SB_KBLK

    xp = x_prompt
    xs = x_sample.reshape(1, ns, D)
    outs_p = [[], [], [], []]
    outs_s = [[], [], [], []]
    for l in range(depth):
        wl = w_in[l]
        c = 3 * sbw + qlora + kvlora
        wkpe = wl[:, c:c + rope]
        wmain = jnp.concatenate([wl[:, :c], _pad_lanes(wkpe), _pad_lanes(_swap_halves(wkpe))], axis=1).astype(BF16)
        wgates = wl[:, c + rope:].astype(BF16)
        wq = w_q_up[l]
        wqn = wq[:, :, :nope].reshape(qlora, H * nope).astype(BF16)
        wqpe = _pad_lanes(wq[:, :, nope:]).reshape(qlora, H * LANES).astype(BF16)
        wqpes = _pad_lanes(_swap_halves(wq[:, :, nope:])).reshape(qlora, H * LANES).astype(BF16)
        wuk_bd = jnp.einsum('hdc,hg->hdgc', jnp.transpose(w_uk[l], (1, 2, 0)), eye).reshape(H * nope, H * kvlora).astype(BF16)
        wuv_bd = jnp.einsum('hcd,hg->hcgd', jnp.transpose(w_uv[l], (1, 0, 2)), eye).reshape(H * kvlora, H * vdim).astype(BF16)
        wr = _pad_lanes(jnp.concatenate([w_group[l], w_router[l]], axis=1))
        wrh = wr.astype(BF16)
        wrl = (wr - wrh.astype(F32)).astype(BF16)
        br = _pad_lanes(jnp.concatenate([b_group[l], b_router[l]]).reshape(1, -1))
        wsb = w_sb_branch[l].astype(BF16)
        wmla = w_mla_branch[l].astype(BF16)
        wo = w_out[l].astype(BF16)
        wg_e = w_gate[l].astype(BF16)
        wu_e = w_up[l].astype(BF16)
        wd_e = w_down[l].astype(BF16)
        lnf = row(ln_f)
        last = l == depth - 1
        proj_args = (row(ln1[l]), wmain, row(q_norm[l]), row(kv_norm[l]), wqn, wuk_bd, wqpe, wqpes)
        proj_kw = dict(sb_scale=sb_scale, mla_scale=mla_scale, sbw=sbw, qlora=qlora, kvlora=kvlora, rope=rope)
        merge_args = (row(ln1[l]), wgates, wuv_bd, wsb, wmla, wo, row(ln2[l]), wrh, wrl, br)

        q, k, v, kb, vb, ckv, kpe, kcat, vext, qcat = _proj(xp, cos_p, sin_p, *proj_args, tm=tm_p, **proj_kw)
        sb_o = _sb_prompt(q, kb, vb, tri)
        lat = _mla_prompt(qcat, jnp.swapaxes(kcat, 1, 2), vext, tq=tq_mla, tk=tq_mla, kvlora=kvlora)
        h, xf, comb = _merge(xp.reshape(B * T, D), sb_o.reshape(B * T, sbw), lat.reshape(B * T, H * kvlora), *merge_args,
                             tm=tm_p)
        y = _moe(xf, comb, h, wg_e, wu_e, wd_e, lnf, tm=min(1024, B * T), final_norm=last)
        xp = y.reshape(B, T, D)
        for lst, a in zip(outs_p, (k, v, ckv, kpe)):
            lst.append(a)

        q2, k2, v2, kb2, vb2, ckv2, kpe2, kcat2, _, qcat2 = _proj(xs, cos_s, sin_s, *proj_args, tm=ns, **proj_kw)
        per_stream = lambda a: a.reshape(SBt, SQ, a.shape[-1])
        k_all = _pad_rows(jnp.concatenate([cache_sb_k[l].reshape(SBt, past, sbw).astype(BF16), per_stream(kb2)], axis=1), tk_all)
        v_all = _pad_rows(jnp.concatenate([cache_sb_v[l].reshape(SBt, past, sbw).astype(BF16), per_stream(vb2)], axis=1), tk_all)
        kc_past = jnp.concatenate([cache_mla_ckv[l], _pad_lanes(cache_mla_kpe[l])], axis=-1).astype(BF16)
        kcat_all = _pad_rows(jnp.concatenate([kc_past, per_stream(kcat2)], axis=1), tk_all)
        sb_o2 = _sb_sample(per_stream(q2), k_all, v_all, tri, qpos0=past)
        qcat_s = jnp.transpose(qcat2.reshape(H, SBt, SQ, kvlora + LANES), (1, 0, 2, 3))
        lat2 = _mla_sample(qcat_s, kcat_all, qpos0=past, nvalid=past + SQ, kvlora=kvlora)
        h2, xf2, comb2 = _merge(xs.reshape(ns, D), sb_o2.reshape(ns, sbw), lat2.reshape(ns, H * kvlora), *merge_args, tm=ns)
        y2 = _moe(xf2, comb2, h2, wg_e, wu_e, wd_e, lnf, tm=ns, final_norm=last)
        xs = y2.reshape(1, ns, D)
        for lst, a in zip(outs_s, (k2, v2, ckv2, kpe2)):
            lst.append(a)

    heads = lambda a, n: a.reshape(n, -1, SB_HEADS, sb_hd)
    return (xp, xs.reshape(SBt, SQ, D),
            jnp.stack([heads(a, B) for a in outs_p[0]]), jnp.stack([heads(a, B) for a in outs_p[1]]),
            jnp.stack(outs_p[2]), jnp.stack(outs_p[3]),
            jnp.stack([heads(a.reshape(SBt, SQ, sbw), SBt) for a in outs_s[0]]),
            jnp.stack([heads(a.reshape(SBt, SQ, sbw), SBt) for a in outs_s[1]]),
            jnp.stack([a.reshape(SBt, SQ, kvlora) for a in outs_s[2]]),
            jnp.stack([a.reshape(SBt, SQ, rope) for a in outs_s[3]]))
```

```python
import functools
import math

import jax
import jax.numpy as jnp
from jax import lax
from jax.experimental import pallas as pl
from jax.experimental.pallas import tpu as pltpu

F32 = jnp.float32
BF16 = jnp.bfloat16

CHUNK = 64
CHUNK_SHIFT = CHUNK.bit_length() - 1
assert 1 << CHUNK_SHIFT == CHUNK
SB_HEADS = 8
MLA_HEADS = 8
ROPE_BASE = 10000.0
N_GROUPS = 4
EXPERTS_PER_GROUP = 8
N_EXPERTS = N_GROUPS * EXPERTS_PER_GROUP
RMS_EPS = 1e-6

LANES = 128
SB_KBLK = LANES
SB_DEAD_CARRY = -110.0
MLA_ROW_GROUPS = 2
MLA_ROW_CHUNK = 64
LOG2E = math.log2(math.e)
VMEM_LIMIT = 56 * 1024 * 1024
ROUTER_LANE0 = N_GROUPS
MOE_GROUP_TILE = 1024
MOE_GROUPED_MIN_TOKENS = 2048


def _dot(a, b):
    return jnp.dot(a, b, preferred_element_type=F32)


def _dot_nt(a, b):
    return lax.dot_general(a, b, (((1,), (1,)), ((), ())), preferred_element_type=F32)


def _rms(x, g):
    return x * lax.rsqrt(jnp.mean(x * x, axis=-1, keepdims=True) + RMS_EPS) * g


def _params(sem):
    return pltpu.CompilerParams(dimension_semantics=sem, vmem_limit_bytes=VMEM_LIMIT)


def _const_spec(shape):
    return pl.BlockSpec(shape, lambda *_: (0,) * len(shape))


def _proj_kernel(x_ref, cos_ref, sin_ref, ln1_ref, wmain_ref, qn_ref, kvn_ref, wqn_ref, wuk_ref, wqpe_ref, wqpes_ref,
                 q_ref, k_ref, v_ref, kb_ref, vb_ref, ckv_ref, kpe_ref, kcat_ref, vext_ref, qcat_ref, *, sb_scale, mla_scale, sbw,
                 qlora, kvlora, rope):
    xn = _rms(x_ref[0], ln1_ref[...]).astype(BF16)
    p = _dot(xn, wmain_ref[...])
    o = 0
    q = p[:, o:o + sbw]; o += sbw
    k = p[:, o:o + sbw]; o += sbw
    v = p[:, o:o + sbw]; o += sbw
    cq = p[:, o:o + qlora]; o += qlora
    ckv = p[:, o:o + kvlora]; o += kvlora
    kpe = p[:, o:o + LANES]; o += LANES
    kpe_sw = p[:, o:o + LANES]
    q_ref[0] = (q * sb_scale).astype(BF16)
    k_ref[0] = k
    v_ref[0] = v
    kb_ref[0] = k.astype(BF16)
    vb_ref[0] = v.astype(BF16)
    cos = cos_ref[...]
    sin = sin_ref[...]
    ckvn = _rms(ckv, kvn_ref[...])
    ckv_ref[0] = ckvn
    kpe_rot = kpe * cos + kpe_sw * sin
    kpe_ref[0] = kpe_rot[:, :rope]
    kcat_ref[0] = jnp.concatenate([ckvn, kpe_rot], axis=-1).astype(BF16)
    vext_ref[0] = jnp.concatenate([ckvn, jnp.ones_like(ckvn)], axis=-1).astype(BF16)
    cqn = _rms(cq, qn_ref[...]).astype(BF16)
    qnope = _dot(cqn, wqn_ref[...]).astype(BF16)
    qlat = _dot(qnope, wuk_ref[...])
    qpe = _dot(cqn, wqpe_ref[...])
    qpe_sw = _dot(cqn, wqpes_ref[...])
    for h in range(MLA_HEADS):
        sl = slice(h * LANES, (h + 1) * LANES)
        qcat_ref[0, h, :, :kvlora] = (qlat[:, h * kvlora:(h + 1) * kvlora] * mla_scale).astype(BF16)
        qcat_ref[0, h, :, kvlora:] = ((qpe[:, sl] * cos + qpe_sw[:, sl] * sin) * mla_scale).astype(BF16)


def _proj(x, cos_t, sin_t, ln1, wmain, qn, kvn, wqn, wuk, wqpe, wqpes, *, tm, sb_scale, mla_scale, sbw, qlora, kvlora, rope):
    B, T, D = x.shape
    grid = (B, T // tm)
    tok = lambda w: pl.BlockSpec((1, tm, w), lambda b, i: (b, i, 0))
    tab = pl.BlockSpec((tm, LANES), lambda b, i: (i, 0))
    kcw = kvlora + LANES
    out_shape = (
        jax.ShapeDtypeStruct((B, T, sbw), BF16),
        jax.ShapeDtypeStruct((B, T, sbw), F32),
        jax.ShapeDtypeStruct((B, T, sbw), F32),
        jax.ShapeDtypeStruct((B, T, sbw), BF16),
        jax.ShapeDtypeStruct((B, T, sbw), BF16),
        jax.ShapeDtypeStruct((B, T, kvlora), F32),
        jax.ShapeDtypeStruct((B, T, rope), F32),
        jax.ShapeDtypeStruct((B, T, kcw), BF16),
        jax.ShapeDtypeStruct((B, T, 2 * kvlora), BF16),
        jax.ShapeDtypeStruct((B, MLA_HEADS, T, kcw), BF16),
    )
    out_specs = (tok(sbw), tok(sbw), tok(sbw), tok(sbw), tok(sbw), tok(kvlora), tok(rope), tok(kcw), tok(2 * kvlora),
                 pl.BlockSpec((1, MLA_HEADS, tm, kcw), lambda b, i: (b, 0, i, 0)))
    in_specs = [tok(D), tab, tab] + [_const_spec(a.shape) for a in (ln1, wmain, qn, kvn, wqn, wuk, wqpe, wqpes)]
    kern = functools.partial(_proj_kernel, sb_scale=sb_scale, mla_scale=mla_scale, sbw=sbw, qlora=qlora, kvlora=kvlora,
                             rope=rope)
    return pl.pallas_call(kern, out_shape=out_shape, grid=grid, in_specs=in_specs, out_specs=out_specs,
                          compiler_params=_params(("parallel", "parallel")), name="proj")(
        x, cos_t, sin_t, ln1, wmain, qn, kvn, wqn, wuk, wqpe, wqpes)


def _head_pair_blockdiag(blk):
    lane = lax.broadcasted_iota(jnp.int32, blk.shape, 1)
    zero = jnp.zeros_like(blk)
    half = LANES // 2
    return jnp.concatenate([jnp.where(lane < half, blk, zero), jnp.where(lane >= half, blk, zero)], axis=0)


def _sb_step(qs, kblk, vblk, tri, carries, accs, keep):
    tq = qs[0].shape[0]
    lbs, loms, parts = [], [], []
    for p, q in enumerate(qs):
        z = _dot_nt(q, _head_pair_blockdiag(kblk[:, p * LANES:(p + 1) * LANES]))
        lb = jnp.minimum(z, 0.0) - jnp.log(1.0 + jnp.exp(-jnp.abs(z)))
        lom = lb - z
        if keep is not None:
            lom = jnp.where(keep, lom, 0.0)
        hi = lom.astype(BF16)
        lbs.append(lb)
        loms.append(lom)
        parts += [hi, (lom - hi.astype(F32)).astype(BF16)]
    r = _dot(jnp.concatenate(parts, axis=0), tri)
    new_c, new_a = [], []
    for p in range(len(qs)):
        incl = r[2 * p * tq:(2 * p + 1) * tq] + r[(2 * p + 1) * tq:(2 * p + 2) * tq]
        w = jnp.exp(lbs[p] + (incl - loms[p]) + carries[p])
        if keep is not None:
            w = jnp.where(keep, w, 0.0)
        new_a.append(accs[p] + _dot(w.astype(BF16), _head_pair_blockdiag(vblk[:, p * LANES:(p + 1) * LANES])))
        tot = jnp.concatenate([jnp.broadcast_to(incl[:, 0:1], (tq, SB_KBLK)),
                               jnp.broadcast_to(incl[:, SB_KBLK:SB_KBLK + 1], (tq, SB_KBLK))], axis=1)
        new_c.append(carries[p] + tot)
    return new_c, new_a


def _sb_prompt_kernel(q_ref, k_ref, v_ref, tri_ref, o_ref, carry_ref, acc_ref, *, tq):
    i = pl.program_id(1)
    npair = q_ref.shape[2] // LANES
    qs = [q_ref[0, :, p * LANES:(p + 1) * LANES] for p in range(npair)]
    tri = tri_ref[...]
    row = lax.broadcasted_iota(jnp.int32, (tq, 2 * SB_KBLK), 0)
    col = lax.broadcasted_iota(jnp.int32, (tq, 2 * SB_KBLK), 1)
    ks = pl.multiple_of(i * SB_KBLK, SB_KBLK)
    zeros_c = [jnp.zeros((tq, 2 * SB_KBLK), F32)] * npair
    zeros_a = [jnp.zeros((tq, LANES), F32)] * npair
    c, a = _sb_step(qs, k_ref[0, pl.ds(ks, SB_KBLK), :], v_ref[0, pl.ds(ks, SB_KBLK), :], tri, zeros_c, zeros_a,
                    (col & (SB_KBLK - 1)) < row)
    for p in range(npair):
        carry_ref[p] = c[p]
        acc_ref[p] = a[p]

    def cond(s):
        j, live = s
        return jnp.logical_and(j >= 0, live > SB_DEAD_CARRY)

    def body(s):
        j, _ = s
        ks = pl.multiple_of(j * SB_KBLK, SB_KBLK)
        c, a = _sb_step(qs, k_ref[0, pl.ds(ks, SB_KBLK), :], v_ref[0, pl.ds(ks, SB_KBLK), :], tri,
                        [carry_ref[p] for p in range(npair)], [acc_ref[p] for p in range(npair)], None)
        for p in range(npair):
            carry_ref[p] = c[p]
            acc_ref[p] = a[p]
        return j - 1, jnp.max(functools.reduce(jnp.maximum, c))

    lax.while_loop(cond, body, (i - 1, jnp.max(functools.reduce(jnp.maximum, c))))
    for p in range(npair):
        o_ref[0, :, p * LANES:(p + 1) * LANES] = acc_ref[p].astype(o_ref.dtype)


def _sb_prompt(q, kb, vb, tri):
    B, T, W = q.shape
    tq = SB_KBLK
    npair = W // LANES
    qspec = pl.BlockSpec((1, tq, W), lambda b, i: (b, i, 0))
    kspec = pl.BlockSpec((1, T, W), lambda b, i: (b, 0, 0))
    return pl.pallas_call(
        functools.partial(_sb_prompt_kernel, tq=tq), out_shape=jax.ShapeDtypeStruct((B, T, W), BF16), grid=(B, T // tq),
        in_specs=[qspec, kspec, kspec, _const_spec(tri.shape)], out_specs=qspec,
        scratch_shapes=[pltpu.VMEM((npair, tq, 2 * SB_KBLK), F32), pltpu.VMEM((npair, tq, LANES), F32)],
        compiler_params=_params(("parallel", "arbitrary")), name="sb_prompt")(q, kb, vb, tri)


def _sb_sample_kernel(q_ref, k_ref, v_ref, tri_ref, o_ref, *, tq, nkb, qpos0):
    npair = q_ref.shape[2] // LANES
    qs = [q_ref[0, :, p * LANES:(p + 1) * LANES] for p in range(npair)]
    tri = tri_ref[...]
    row = lax.broadcasted_iota(jnp.int32, (tq, 2 * SB_KBLK), 0)
    col = lax.broadcasted_iota(jnp.int32, (tq, 2 * SB_KBLK), 1)
    c = [jnp.zeros((tq, 2 * SB_KBLK), F32)] * npair
    a = [jnp.zeros((tq, LANES), F32)] * npair
    for j in range(nkb - 1, -1, -1):
        ks = j * SB_KBLK
        c, a = _sb_step(qs, k_ref[0, ks:ks + SB_KBLK, :], v_ref[0, ks:ks + SB_KBLK, :], tri, c, a,
                        (ks + (col & (SB_KBLK - 1))) < (qpos0 + row))
    for p in range(npair):
        o_ref[0, :, p * LANES:(p + 1) * LANES] = a[p].astype(o_ref.dtype)


def _sb_sample(q, k_all, v_all, tri, *, qpos0):
    B, tq, W = q.shape
    Tk = k_all.shape[1]
    qspec = pl.BlockSpec((1, tq, W), lambda b: (b, 0, 0))
    kspec = pl.BlockSpec((1, Tk, W), lambda b: (b, 0, 0))
    return pl.pallas_call(
        functools.partial(_sb_sample_kernel, tq=tq, nkb=Tk // SB_KBLK, qpos0=qpos0),
        out_shape=jax.ShapeDtypeStruct((B, tq, W), BF16), grid=(B,),
        in_specs=[qspec, kspec, kspec, _const_spec(tri.shape)], out_specs=qspec,
        compiler_params=_params(("parallel",)), name="sb_sample")(q, k_all, v_all, tri)


def _mla_prompt_kernel(q_ref, kt_ref, vx_ref, o_ref, s_ref, p_ref, m_ref, a_ref, acc_ref, *, tq, tk, kvlora):
    i = pl.program_id(1)
    H = q_ref.shape[1]
    kc = q_ref.shape[3]
    group_heads = H // MLA_ROW_GROUPS
    gr = group_heads * tq
    m_ref[...] = jnp.full(m_ref.shape, -jnp.inf, F32)
    acc_ref[...] = jnp.zeros(acc_ref.shape, F32)

    def step(j, masked):
        ks = pl.multiple_of(j * tk, tk)
        kt = kt_ref[0, :, pl.ds(ks, tk)]
        vx = vx_ref[0, pl.ds(ks, tk), :]
        for g in range(MLA_ROW_GROUPS):
            qg = q_ref[0, g * group_heads:(g + 1) * group_heads].reshape(gr, kc)
            s_ref[g * gr:(g + 1) * gr, :] = _dot(qg, kt)
        for g in range(MLA_ROW_GROUPS):
            for c in range(gr // MLA_ROW_CHUNK):
                r0 = g * gr + c * MLA_ROW_CHUNK
                rs = slice(r0, r0 + MLA_ROW_CHUNK)
                tiles = [s_ref[rs, t * LANES:(t + 1) * LANES] for t in range(tk // LANES)]
                if masked:
                    row = lax.broadcasted_iota(jnp.int32, (MLA_ROW_CHUNK, LANES), 0)
                    col = lax.broadcasted_iota(jnp.int32, (MLA_ROW_CHUNK, LANES), 1)
                    qchunk = (i * tq + (r0 % tq) + row) >> CHUNK_SHIFT
                    tiles = [jnp.where(((j * tk + t * LANES + col) >> CHUNK_SHIFT) <= qchunk, st, -jnp.inf)
                             for t, st in enumerate(tiles)]
                smax = functools.reduce(jnp.maximum, tiles)
                m_old = m_ref[rs, :]
                m_new = jnp.maximum(m_old, jnp.max(smax, axis=-1, keepdims=True))
                for t, st in enumerate(tiles):
                    p_ref[rs, t * LANES:(t + 1) * LANES] = jnp.exp2(st - m_new).astype(BF16)
                a_ref[rs, :] = jnp.exp2(m_old - m_new)
                m_ref[rs, :] = m_new
            gs = slice(g * gr, (g + 1) * gr)
            alpha = a_ref[gs, :]
            acc_ref[gs, :] = jnp.concatenate([alpha, alpha], axis=-1) * acc_ref[gs, :] + _dot(p_ref[gs, :], vx)

    nfull = (i * tq) // tk

    def body(j, c):
        step(j, False)
        return c

    lax.fori_loop(0, nfull, body, 0)
    for d in range(tq // tk):
        step(nfull + d, True)
    for h in range(H):
        hs = slice(h * tq, (h + 1) * tq)
        o_ref[0, :, h * kvlora:(h + 1) * kvlora] = (acc_ref[hs, :kvlora] / acc_ref[hs, kvlora:]).astype(o_ref.dtype)


def _mla_prompt(qcat, kcat_t, vext, *, tq, tk, kvlora):
    B, H, T, KC = qcat.shape
    grid = (B, T // tq)
    rows = H * tq
    return pl.pallas_call(
        functools.partial(_mla_prompt_kernel, tq=tq, tk=tk, kvlora=kvlora),
        out_shape=jax.ShapeDtypeStruct((B, T, H * kvlora), BF16), grid=grid,
        in_specs=[pl.BlockSpec((1, H, tq, KC), lambda b, i: (b, 0, i, 0)), pl.BlockSpec((1, KC, T), lambda b, i: (b, 0, 0)),
                  pl.BlockSpec((1, T, 2 * kvlora), lambda b, i: (b, 0, 0))],
        out_specs=pl.BlockSpec((1, tq, H * kvlora), lambda b, i: (b, i, 0)),
        scratch_shapes=[pltpu.VMEM((rows, tk), F32), pltpu.VMEM((rows, tk), BF16), pltpu.VMEM((rows, LANES), F32),
                        pltpu.VMEM((rows, LANES), F32), pltpu.VMEM((rows, 2 * kvlora), F32)],
        compiler_params=_params(("parallel", "arbitrary")), name="mla_prompt")(qcat, kcat_t, vext)


def _mla_sample_kernel(q_ref, kc_ref, o_ref, *, tq, qpos0, nvalid, kvlora):
    H = q_ref.shape[1]
    q = q_ref[0].reshape(H * tq, q_ref.shape[3])
    kall = kc_ref[0]
    s = _dot_nt(q, kall)
    row = lax.broadcasted_iota(jnp.int32, s.shape, 0)
    col = lax.broadcasted_iota(jnp.int32, s.shape, 1)
    qpos = qpos0 + (row & (tq - 1))
    keep = jnp.logical_and((col >> CHUNK_SHIFT) <= (qpos >> CHUNK_SHIFT), col < nvalid)
    s = jnp.where(keep, s, -jnp.inf)
    p = jnp.exp2(s - jnp.max(s, axis=-1, keepdims=True))
    out = _dot(p.astype(BF16), kall[:, :kvlora]) / jnp.sum(p, axis=-1, keepdims=True)
    for h in range(H):
        o_ref[0, :, h * kvlora:(h + 1) * kvlora] = out[h * tq:(h + 1) * tq].astype(o_ref.dtype)


def _mla_sample(qcat, kcat_all, *, qpos0, nvalid, kvlora):
    B, H, tq, KC = qcat.shape
    Tk = kcat_all.shape[1]
    return pl.pallas_call(
        functools.partial(_mla_sample_kernel, tq=tq, qpos0=qpos0, nvalid=nvalid, kvlora=kvlora),
        out_shape=jax.ShapeDtypeStruct((B, tq, H * kvlora), BF16), grid=(B,),
        in_specs=[pl.BlockSpec((1, H, tq, KC), lambda b: (b, 0, 0, 0)), pl.BlockSpec((1, Tk, KC), lambda b: (b, 0, 0))],
        out_specs=pl.BlockSpec((1, tq, H * kvlora), lambda b: (b, 0, 0)),
        compiler_params=_params(("parallel",)), name="mla_sample")(qcat, kcat_all)


def _router_logits(xf_bf, wrh_ref, wrl_ref, br_ref):
    return _dot(xf_bf, wrh_ref[...]) + _dot(xf_bf, wrl_ref[...]) + br_ref[...]


def _route(logits, gidx=None):
    lane = lax.broadcasted_iota(jnp.int32, logits.shape, 1)
    ninf = -jnp.inf
    gl = jnp.where(lane < N_GROUPS, logits, ninf)
    gmax = jnp.max(gl, axis=-1, keepdims=True)
    gexp = jnp.where(lane < N_GROUPS, jnp.exp(logits - gmax), 0.0)
    if gidx is None:
        gidx = jnp.min(jnp.where(gl == gmax, lane, LANES), axis=-1, keepdims=True)
    g_val = jnp.sum(jnp.where(lane == gidx, gexp, 0.0), axis=-1, keepdims=True) / jnp.sum(gexp, axis=-1, keepdims=True)
    lo = ROUTER_LANE0 + EXPERTS_PER_GROUP * gidx
    el = jnp.where(jnp.logical_and(lane >= lo, lane < lo + EXPERTS_PER_GROUP), logits, ninf)
    v1 = jnp.max(el, axis=-1, keepdims=True)
    i1 = jnp.min(jnp.where(el == v1, lane, LANES), axis=-1, keepdims=True)
    el2 = jnp.where(lane == i1, ninf, el)
    v2 = jnp.max(el2, axis=-1, keepdims=True)
    i2 = jnp.min(jnp.where(el2 == v2, lane, LANES), axis=-1, keepdims=True)
    e2 = jnp.exp(v2 - v1)
    den = 1.0 + e2
    return jnp.where(lane == i1, (1.0 / den) * g_val, 0.0) + jnp.where(lane == i2, (e2 / den) * g_val, 0.0), gidx


def _merge_kernel(x_ref, sb_ref, lat_ref, ln1_ref, wg_ref, wuv_ref, wsb_ref, wmla_ref, wout_ref, ln2_ref, wrh_ref, wrl_ref,
                  br_ref, tril_ref, h_ref, xf_ref, comb_ref, info_ref, cnt_ref, *, d):
    @pl.when(pl.program_id(0) == 0)
    def _():
        cnt_ref[...] = jnp.zeros(cnt_ref.shape, F32)

    x = x_ref[...]
    xn = _rms(x, ln1_ref[...]).astype(BF16)
    g = _dot(xn, wg_ref[...])
    mla_o = _dot(lat_ref[...], wuv_ref[...]).astype(BF16)
    merged = jax.nn.sigmoid(g[:, :d]) * _dot(sb_ref[...], wsb_ref[...]) + jax.nn.sigmoid(g[:, d:]) * _dot(mla_o, wmla_ref[...])
    h = x + _dot(merged.astype(BF16), wout_ref[...])
    h_ref[...] = h
    xf = _rms(h, ln2_ref[...]).astype(BF16)
    xf_ref[...] = xf
    comb, gidx = _route(_router_logits(xf, wrh_ref, wrl_ref, br_ref))
    comb_ref[...] = comb
    lane = lax.broadcasted_iota(jnp.int32, comb.shape, 1)
    onehot = lane == gidx
    before = _dot(tril_ref[...], jnp.where(onehot, 1.0, 0.0).astype(BF16)) + cnt_ref[...]
    rank = jnp.sum(jnp.where(onehot, before, 0.0), axis=-1, keepdims=True).astype(jnp.int32)
    info_ref[...] = jnp.where(lane == 0, rank, jnp.where(lane == 1, gidx, 0))
    cnt_ref[...] += jnp.sum(jnp.where(onehot, 1.0, 0.0), axis=0, keepdims=True)


def _merge(x, sb_o, lat, ln1, wg, wuv, wsb, wmla, wout, ln2, wrh, wrl, br, tril, *, tm):
    N, D = x.shape
    tok = lambda w: pl.BlockSpec((tm, w), lambda i: (i, 0))
    consts = (ln1, wg, wuv, wsb, wmla, wout, ln2, wrh, wrl, br, tril)
    return pl.pallas_call(
        functools.partial(_merge_kernel, d=D),
        out_shape=(jax.ShapeDtypeStruct((N, D), F32), jax.ShapeDtypeStruct((N, D), BF16), jax.ShapeDtypeStruct((N, LANES), F32),
                   jax.ShapeDtypeStruct((N, LANES), jnp.int32), jax.ShapeDtypeStruct((1, LANES), F32)),
        grid=(N // tm,), in_specs=[tok(D), tok(sb_o.shape[1]), tok(lat.shape[1])] + [_const_spec(a.shape) for a in consts],
        out_specs=(tok(D), tok(D), tok(LANES), tok(LANES), _const_spec((1, LANES))),
        compiler_params=_params(("arbitrary",)), name="merge")(x, sb_o, lat, *consts)


def _moe_kernel(xf_ref, comb_ref, h_ref, wg_ref, wu_ref, wd_ref, lnf_ref, y_ref, acc_ref, *, final_norm):
    e = pl.program_id(1)

    @pl.when(e == 0)
    def _():
        acc_ref[...] = jnp.zeros(acc_ref.shape, F32)

    xf = xf_ref[...]
    comb = comb_ref[...]
    lane = lax.broadcasted_iota(jnp.int32, comb.shape, 1)
    c = jnp.sum(jnp.where(lane == e + ROUTER_LANE0, comb, 0.0), axis=-1, keepdims=True)
    act = jax.nn.silu(_dot(xf, wg_ref[0])) * _dot(xf, wu_ref[0])
    acc_ref[...] += _dot((act * c).astype(BF16), wd_ref[0])

    @pl.when(e == pl.num_programs(1) - 1)
    def _():
        y = h_ref[...] + acc_ref[...]
        y_ref[...] = _rms(y, lnf_ref[...]) if final_norm else y


def _moe(xf, comb, h, wg, wu, wd, lnf, *, tm, final_norm):
    N, D = h.shape
    E, _, F = wg.shape
    tok = lambda w: pl.BlockSpec((tm, w), lambda i, e: (i, 0))
    return pl.pallas_call(
        functools.partial(_moe_kernel, final_norm=final_norm), out_shape=jax.ShapeDtypeStruct((N, D), F32), grid=(N // tm, E),
        in_specs=[tok(D), tok(LANES), tok(D), pl.BlockSpec((1, D, F), lambda i, e: (e, 0, 0)),
                  pl.BlockSpec((1, D, F), lambda i, e: (e, 0, 0)), pl.BlockSpec((1, F, D), lambda i, e: (e, 0, 0)),
                  pl.BlockSpec((1, D), lambda i, e: (0, 0))],
        out_specs=tok(D), scratch_shapes=[pltpu.VMEM((tm, D), F32)],
        compiler_params=_params(("parallel", "arbitrary")), name="moe")(xf, comb, h, wg, wu, wd, lnf)


def _to_row_tiles(ref, x):
    for k in range(ref.shape[1]):
        ref[:, k, :] = x[:, k * LANES:(k + 1) * LANES]


def _from_row_tiles(ref):
    return jnp.concatenate([ref[:, k, :] for k in range(ref.shape[1])], axis=-1)


def _row_dma_loop(n, make_copy):
    def issue(r, c):
        make_copy(r).start()
        return c

    def drain(r, c):
        make_copy(0).wait()
        return c

    lax.fori_loop(0, n, issue, 0, unroll=8)
    lax.fori_loop(0, n, drain, 0, unroll=8)


def _dispatch_kernel(pos_hbm, xf_ref, init_hbm, out_hbm, idx_ref, buf_ref, sem_idx, sem_rows, *, tm):
    del init_hbm
    idx_copy = pltpu.make_async_copy(pos_hbm.at[pl.program_id(0)], idx_ref, sem_idx)
    idx_copy.start()
    _to_row_tiles(buf_ref, xf_ref[...].astype(F32))
    idx_copy.wait()
    _row_dma_loop(tm, lambda r: pltpu.make_async_copy(buf_ref.at[r], out_hbm.at[idx_ref[0, r]], sem_rows))


def _dispatch(pos, xf, n_rows, *, tm):
    N, D = xf.shape
    tiles = D // LANES
    init = jnp.zeros((n_rows, tiles, LANES), F32)
    anyspec = pl.BlockSpec(memory_space=pl.ANY)
    return pl.pallas_call(
        functools.partial(_dispatch_kernel, tm=tm), out_shape=jax.ShapeDtypeStruct(init.shape, F32), grid=(N // tm,),
        in_specs=[anyspec, pl.BlockSpec((tm, D), lambda i: (i, 0)), anyspec], out_specs=anyspec,
        scratch_shapes=[pltpu.SMEM((1, tm), jnp.int32), pltpu.VMEM((tm, tiles, LANES), F32), pltpu.SemaphoreType.DMA,
                        pltpu.SemaphoreType.DMA],
        input_output_aliases={2: 0}, compiler_params=_params(("arbitrary",)), name="moe_dispatch")(
        pos.reshape(N // tm, 1, tm), xf, init)


def _moe_grouped_kernel(grp_ref, valid_ref, xs_ref, wg_ref, wu_ref, wd_ref, wrh_ref, wrl_ref, br_ref, ys_ref,
                        x_ref, comb_ref, acc_ref):
    w = pl.program_id(0)
    e = pl.program_id(1)
    grp = grp_ref[w]

    @pl.when(jnp.logical_and(valid_ref[w] == 0, e == 0))
    def _():
        ys_ref[...] = jnp.zeros(ys_ref.shape, F32)

    @pl.when(valid_ref[w] == 1)
    def _():
        @pl.when(e == 0)
        def _():
            x = _from_row_tiles(xs_ref).astype(BF16)
            x_ref[...] = x
            comb_ref[...] = _route(_router_logits(x, wrh_ref, wrl_ref, br_ref), grp)[0]
            acc_ref[...] = jnp.zeros(acc_ref.shape, F32)

        x = x_ref[...]
        comb = comb_ref[...]
        lane = lax.broadcasted_iota(jnp.int32, comb.shape, 1)
        c = jnp.sum(jnp.where(lane == ROUTER_LANE0 + EXPERTS_PER_GROUP * grp + e, comb, 0.0), axis=-1, keepdims=True)
        act = jax.nn.silu(_dot(x, wg_ref[0])) * _dot(x, wu_ref[0])
        acc_ref[...] += _dot((act * c).astype(BF16), wd_ref[0])

        @pl.when(e == pl.num_programs(1) - 1)
        def _():
            _to_row_tiles(ys_ref, acc_ref[...])


def _moe_grouped(grp, valid, xs, wg, wu, wd, wrh, wrl, br, *, tm):
    S, tiles, _ = xs.shape
    D = tiles * LANES
    F = wg.shape[2]
    rows = pl.BlockSpec((tm, tiles, LANES), lambda w, e, grp, valid: (w, 0, 0))
    last = EXPERTS_PER_GROUP - 1
    expert = lambda w, e, grp, valid: (EXPERTS_PER_GROUP * grp[w] + e * valid[w] + last * (1 - valid[w]), 0, 0)
    const = lambda shape: pl.BlockSpec(shape, lambda w, e, grp, valid: (0,) * len(shape))
    grid_spec = pltpu.PrefetchScalarGridSpec(
        num_scalar_prefetch=2, grid=(S // tm, EXPERTS_PER_GROUP),
        in_specs=[rows, pl.BlockSpec((1, D, F), expert), pl.BlockSpec((1, D, F), expert), pl.BlockSpec((1, F, D), expert),
                  const(wrh.shape), const(wrl.shape), const(br.shape)],
        out_specs=rows,
        scratch_shapes=[pltpu.VMEM((tm, D), BF16), pltpu.VMEM((tm, LANES), F32), pltpu.VMEM((tm, D), F32)])
    return pl.pallas_call(_moe_grouped_kernel, out_shape=jax.ShapeDtypeStruct(xs.shape, F32), grid_spec=grid_spec,
                          compiler_params=_params(("arbitrary", "arbitrary")), name="moe_grouped")(
        grp, valid, xs, wg, wu, wd, wrh, wrl, br)


def _combine_kernel(pos_hbm, h_ref, lnf_ref, ys_hbm, y_ref, idx_ref, buf_ref, sem_idx, sem_rows, *, tm, final_norm):
    idx_copy = pltpu.make_async_copy(pos_hbm.at[pl.program_id(0)], idx_ref, sem_idx)
    idx_copy.start()
    idx_copy.wait()
    _row_dma_loop(tm, lambda r: pltpu.make_async_copy(ys_hbm.at[idx_ref[0, r]], buf_ref.at[r], sem_rows))
    y = h_ref[...] + _from_row_tiles(buf_ref)
    y_ref[...] = _rms(y, lnf_ref[...]) if final_norm else y


def _combine(pos, h, lnf, ys, *, tm, final_norm):
    N, D = h.shape
    tiles = D // LANES
    anyspec = pl.BlockSpec(memory_space=pl.ANY)
    return pl.pallas_call(
        functools.partial(_combine_kernel, tm=tm, final_norm=final_norm), out_shape=jax.ShapeDtypeStruct((N, D), F32),
        grid=(N // tm,), in_specs=[anyspec, pl.BlockSpec((tm, D), lambda i: (i, 0)), _const_spec(lnf.shape), anyspec],
        out_specs=pl.BlockSpec((tm, D), lambda i: (i, 0)),
        scratch_shapes=[pltpu.SMEM((1, tm), jnp.int32), pltpu.VMEM((tm, tiles, LANES), F32), pltpu.SemaphoreType.DMA,
                        pltpu.SemaphoreType.DMA],
        compiler_params=_params(("arbitrary",)), name="moe_combine")(pos.reshape(N // tm, 1, tm), h, lnf, ys)


def _group_layout(info, cnt, *, tm, n_tiles):
    rank, gidx = info[:, 0], info[:, 1]
    counts = cnt[0, :N_GROUPS].astype(jnp.int32)
    ntiles = (counts + tm - 1) // tm
    tile_end = jnp.cumsum(ntiles)
    pos = (tile_end - ntiles)[gidx] * tm + rank
    tile = jnp.arange(n_tiles, dtype=jnp.int32)
    valid = (tile < tile_end[-1]).astype(jnp.int32)
    grp = jnp.sum((jnp.minimum(tile, tile_end[-1] - 1)[:, None] >= tile_end[None, :]).astype(jnp.int32), axis=1)
    return pos, grp, valid


def _pad_lanes(w):
    return jnp.pad(w, [(0, 0)] * (w.ndim - 1) + [(0, LANES - w.shape[-1])])


def _swap_halves(w):
    half = w.shape[-1] // 2
    return jnp.concatenate([w[..., half:], w[..., :half]], axis=-1)


def _rope_tables(pos, rope):
    half = rope // 2
    inv_freq = ROPE_BASE ** (-jnp.arange(half, dtype=F32) / half)
    ang = pos.astype(F32)[:, None] * inv_freq[None, :]
    cos, sin = jnp.cos(ang), jnp.sin(ang)
    return _pad_lanes(jnp.concatenate([cos, cos], axis=-1)), _pad_lanes(jnp.concatenate([-sin, sin], axis=-1))


def _sb_scan_operator():
    n = SB_KBLK
    tri = (jnp.arange(n)[:, None] >= jnp.arange(n)[None, :]).astype(F32)
    z = jnp.zeros((n, n), F32)
    return jnp.concatenate([jnp.concatenate([tri, z], axis=1), jnp.concatenate([z, tri], axis=1)], axis=0).astype(BF16)


def _strict_lower(n):
    return (jnp.arange(n)[:, None] > jnp.arange(n)[None, :]).astype(BF16)


def _pad_rows(a, rows):
    return jnp.pad(a, [(0, 0), (0, rows - a.shape[1]), (0, 0)])


def kernel(x_prompt, x_sample, cache_sb_k, cache_sb_v, cache_mla_ckv, cache_mla_kpe, ln1, w_in, q_norm, w_q_up, kv_norm, w_uk,
           w_uv, w_sb_branch, w_mla_branch, w_out, ln2, w_group, b_group, w_router, b_router, w_gate, w_up, w_down, ln_f):
    B, T, D = x_prompt.shape
    SBt, SQ, _ = x_sample.shape
    depth = ln1.shape[0]
    past = cache_sb_k.shape[2]
    sbw = cache_sb_k.shape[3] * cache_sb_k.shape[4]
    sb_hd = cache_sb_k.shape[4]
    qlora = q_norm.shape[1]
    kvlora = kv_norm.shape[1]
    rope = cache_mla_kpe.shape[3]
    nope = w_uk.shape[3]
    vdim = w_uv.shape[3]
    H = w_uk.shape[2]
    sb_scale = 1.0 / math.sqrt(sb_hd)
    mla_scale = LOG2E / math.sqrt(nope + rope)

    cos_p, sin_p = _rope_tables(jnp.arange(T, dtype=jnp.int32), rope)
    pos_s = past + jnp.arange(SQ, dtype=jnp.int32)
    cos_s, sin_s = _rope_tables(jnp.tile(pos_s, SBt), rope)
    tri = _sb_scan_operator()
    eye = jnp.eye(H, dtype=F32)
    row = lambda a: a.reshape(1, -1)

    tm_p = min(512, T)
    tq_mla = min(512, T)
    ns = SBt * SQ
    tk_all = -(-(past + SQ) // SB_KBLK) * SB_KBLK

    xp = x_prompt
    xs = x_sample.reshape(1, ns, D)
    outs_p = [[], [], [], []]
    outs_s = [[], [], [], []]
    for l in range(depth):
        wl = w_in[l]
        c = 3 * sbw + qlora + kvlora
        wkpe = wl[:, c:c + rope]
        wmain = jnp.concatenate([wl[:, :c], _pad_lanes(wkpe), _pad_lanes(_swap_halves(wkpe))], axis=1).astype(BF16)
        wgates = wl[:, c + rope:].astype(BF16)
        wq = w_q_up[l]
        wqn = wq[:, :, :nope].reshape(qlora, H * nope).astype(BF16)
        wqpe = _pad_lanes(wq[:, :, nope:]).reshape(qlora, H * LANES).astype(BF16)
        wqpes = _pad_lanes(_swap_halves(wq[:, :, nope:])).reshape(qlora, H * LANES).astype(BF16)
        wuk_bd = jnp.einsum('hdc,hg->hdgc', jnp.transpose(w_uk[l], (1, 2, 0)), eye).reshape(H * nope, H * kvlora).astype(BF16)
        wuv_bd = jnp.einsum('hcd,hg->hcgd', jnp.transpose(w_uv[l], (1, 0, 2)), eye).reshape(H * kvlora, H * vdim).astype(BF16)
        wr = _pad_lanes(jnp.concatenate([w_group[l], w_router[l]], axis=1))
        wrh = wr.astype(BF16)
        wrl = (wr - wrh.astype(F32)).astype(BF16)
        br = _pad_lanes(jnp.concatenate([b_group[l], b_router[l]]).reshape(1, -1))
        wsb = w_sb_branch[l].astype(BF16)
        wmla = w_mla_branch[l].astype(BF16)
        wo = w_out[l].astype(BF16)
        wg_e = w_gate[l].astype(BF16)
        wu_e = w_up[l].astype(BF16)
        wd_e = w_down[l].astype(BF16)
        lnf = row(ln_f)
        last = l == depth - 1
        proj_args = (row(ln1[l]), wmain, row(q_norm[l]), row(kv_norm[l]), wqn, wuk_bd, wqpe, wqpes)
        proj_kw = dict(sb_scale=sb_scale, mla_scale=mla_scale, sbw=sbw, qlora=qlora, kvlora=kvlora, rope=rope)
        merge_args = (row(ln1[l]), wgates, wuv_bd, wsb, wmla, wo, row(ln2[l]), wrh, wrl, br)

        def moe_block(h, xf, comb, info, cnt, tm_tok):
            n = h.shape[0]
            tm_g = min(MOE_GROUP_TILE, n // N_GROUPS)
            if n < MOE_GROUPED_MIN_TOKENS:
                return _moe(xf, comb, h, wg_e, wu_e, wd_e, lnf, tm=min(MOE_GROUP_TILE, n), final_norm=last)
            n_tiles = n // tm_g + N_GROUPS
            pos, grp, valid = _group_layout(info, cnt, tm=tm_g, n_tiles=n_tiles)
            xs_rows = _dispatch(pos, xf, n_tiles * tm_g, tm=tm_tok)
            ys_rows = _moe_grouped(grp, valid, xs_rows, wg_e, wu_e, wd_e, wrh, wrl, br, tm=tm_g)
            return _combine(pos, h, lnf, ys_rows, tm=tm_tok, final_norm=last)

        q, k, v, kb, vb, ckv, kpe, kcat, vext, qcat = _proj(xp, cos_p, sin_p, *proj_args, tm=tm_p, **proj_kw)
        sb_o = _sb_prompt(q, kb, vb, tri)
        lat = _mla_prompt(qcat, jnp.swapaxes(kcat, 1, 2), vext, tq=tq_mla, tk=tq_mla, kvlora=kvlora)
        y = moe_block(*_merge(xp.reshape(B * T, D), sb_o.reshape(B * T, sbw), lat.reshape(B * T, H * kvlora), *merge_args,
                              _strict_lower(tm_p), tm=tm_p), tm_p)
        xp = y.reshape(B, T, D)
        for lst, a in zip(outs_p, (k, v, ckv, kpe)):
            lst.append(a)

        q2, k2, v2, kb2, vb2, ckv2, kpe2, kcat2, _, qcat2 = _proj(xs, cos_s, sin_s, *proj_args, tm=ns, **proj_kw)
        per_stream = lambda a: a.reshape(SBt, SQ, a.shape[-1])
        k_all = _pad_rows(jnp.concatenate([cache_sb_k[l].reshape(SBt, past, sbw).astype(BF16), per_stream(kb2)], axis=1), tk_all)
        v_all = _pad_rows(jnp.concatenate([cache_sb_v[l].reshape(SBt, past, sbw).astype(BF16), per_stream(vb2)], axis=1), tk_all)
        kc_past = jnp.concatenate([cache_mla_ckv[l], _pad_lanes(cache_mla_kpe[l])], axis=-1).astype(BF16)
        kcat_all = _pad_rows(jnp.concatenate([kc_past, per_stream(kcat2)], axis=1), tk_all)
        sb_o2 = _sb_sample(per_stream(q2), k_all, v_all, tri, qpos0=past)
        qcat_s = jnp.transpose(qcat2.reshape(H, SBt, SQ, kvlora + LANES), (1, 0, 2, 3))
        lat2 = _mla_sample(qcat_s, kcat_all, qpos0=past, nvalid=past + SQ, kvlora=kvlora)
        y2 = moe_block(*_merge(xs.reshape(ns, D), sb_o2.reshape(ns, sbw), lat2.reshape(ns, H * kvlora), *merge_args,
                              _strict_lower(ns), tm=ns), ns)
        xs = y2.reshape(1, ns, D)
        for lst, a in zip(outs_s, (k2, v2, ckv2, kpe2)):
            lst.append(a)

    heads = lambda a, n: a.reshape(n, -1, SB_HEADS, sb_hd)
    return (xp, xs.reshape(SBt, SQ, D),
            jnp.stack([heads(a, B) for a in outs_p[0]]), jnp.stack([heads(a, B) for a in outs_p[1]]),
            jnp.stack(outs_p[2]), jnp.stack(outs_p[3]),
            jnp.stack([heads(a.reshape(SBt, SQ, sbw), SBt) for a in outs_s[0]]),
            jnp.stack([heads(a.reshape(SBt, SQ, sbw), SBt) for a in outs_s[1]]),
            jnp.stack([a.reshape(SBt, SQ, kvlora) for a in outs_s[2]]),
            jnp.stack([a.reshape(SBt, SQ, rope) for a in outs_s[3]]))
```

```python
import functools
import math

import jax
import jax.numpy as jnp
from jax import lax
from jax.experimental import pallas as pl
from jax.experimental.pallas import tpu as pltpu

F32 = jnp.float32
BF16 = jnp.bfloat16

CHUNK = 64
CHUNK_SHIFT = CHUNK.bit_length() - 1
assert 1 << CHUNK_SHIFT == CHUNK
SB_HEADS = 8
MLA_HEADS = 8
ROPE_BASE = 10000.0
N_GROUPS = 4
EXPERTS_PER_GROUP = 8
N_EXPERTS = N_GROUPS * EXPERTS_PER_GROUP
RMS_EPS = 1e-6

LANES = 128
SB_KBLK = LANES
SB_DEAD_CARRY = -110.0
MLA_ROW_GROUPS = 2
MLA_ROW_CHUNK = 64
LOG2E = math.log2(math.e)
VMEM_LIMIT = 56 * 1024 * 1024
ROUTER_LANE0 = N_GROUPS
MOE_GROUP_TILE = 512
MOE_ROW_DMA_TILE = 2048
MOE_GROUPED_MIN_TOKENS = 2048


def _dot(a, b):
    return jnp.dot(a, b, preferred_element_type=F32)


def _dot_nt(a, b):
    return lax.dot_general(a, b, (((1,), (1,)), ((), ())), preferred_element_type=F32)


def _rms(x, g):
    return x * lax.rsqrt(jnp.mean(x * x, axis=-1, keepdims=True) + RMS_EPS) * g


def _params(sem):
    return pltpu.CompilerParams(dimension_semantics=sem, vmem_limit_bytes=VMEM_LIMIT)


def _const_spec(shape):
    return pl.BlockSpec(shape, lambda *_: (0,) * len(shape))


def _proj_kernel(x_ref, cos_ref, sin_ref, ln1_ref, wmain_ref, qn_ref, kvn_ref, wqn_ref, wuk_ref, wqpe_ref, wqpes_ref,
                 q_ref, k_ref, v_ref, kb_ref, vb_ref, ckv_ref, kpe_ref, kcat_ref, vext_ref, qcat_ref, *, sb_scale, mla_scale, sbw,
                 qlora, kvlora, rope):
    xn = _rms(x_ref[0], ln1_ref[...]).astype(BF16)
    p = _dot(xn, wmain_ref[...])
    o = 0
    q = p[:, o:o + sbw]; o += sbw
    k = p[:, o:o + sbw]; o += sbw
    v = p[:, o:o + sbw]; o += sbw
    cq = p[:, o:o + qlora]; o += qlora
    ckv = p[:, o:o + kvlora]; o += kvlora
    kpe = p[:, o:o + LANES]; o += LANES
    kpe_sw = p[:, o:o + LANES]
    q_ref[0] = (q * sb_scale).astype(BF16)
    k_ref[0] = k
    v_ref[0] = v
    kb_ref[0] = k.astype(BF16)
    vb_ref[0] = v.astype(BF16)
    cos = cos_ref[...]
    sin = sin_ref[...]
    ckvn = _rms(ckv, kvn_ref[...])
    ckv_ref[0] = ckvn
    kpe_rot = kpe * cos + kpe_sw * sin
    kpe_ref[0] = kpe_rot[:, :rope]
    kcat_ref[0] = jnp.concatenate([ckvn, kpe_rot], axis=-1).astype(BF16)
    vext_ref[0] = jnp.concatenate([ckvn, jnp.ones_like(ckvn)], axis=-1).astype(BF16)
    cqn = _rms(cq, qn_ref[...]).astype(BF16)
    qnope = _dot(cqn, wqn_ref[...]).astype(BF16)
    qlat = _dot(qnope, wuk_ref[...])
    qpe = _dot(cqn, wqpe_ref[...])
    qpe_sw = _dot(cqn, wqpes_ref[...])
    for h in range(MLA_HEADS):
        sl = slice(h * LANES, (h + 1) * LANES)
        qcat_ref[0, h, :, :kvlora] = (qlat[:, h * kvlora:(h + 1) * kvlora] * mla_scale).astype(BF16)
        qcat_ref[0, h, :, kvlora:] = ((qpe[:, sl] * cos + qpe_sw[:, sl] * sin) * mla_scale).astype(BF16)


def _proj(x, cos_t, sin_t, ln1, wmain, qn, kvn, wqn, wuk, wqpe, wqpes, *, tm, sb_scale, mla_scale, sbw, qlora, kvlora, rope):
    B, T, D = x.shape
    grid = (B, T // tm)
    tok = lambda w: pl.BlockSpec((1, tm, w), lambda b, i: (b, i, 0))
    tab = pl.BlockSpec((tm, LANES), lambda b, i: (i, 0))
    kcw = kvlora + LANES
    out_shape = (
        jax.ShapeDtypeStruct((B, T, sbw), BF16),
        jax.ShapeDtypeStruct((B, T, sbw), F32),
        jax.ShapeDtypeStruct((B, T, sbw), F32),
        jax.ShapeDtypeStruct((B, T, sbw), BF16),
        jax.ShapeDtypeStruct((B, T, sbw), BF16),
        jax.ShapeDtypeStruct((B, T, kvlora), F32),
        jax.ShapeDtypeStruct((B, T, rope), F32),
        jax.ShapeDtypeStruct((B, T, kcw), BF16),
        jax.ShapeDtypeStruct((B, T, 2 * kvlora), BF16),
        jax.ShapeDtypeStruct((B, MLA_HEADS, T, kcw), BF16),
    )
    out_specs = (tok(sbw), tok(sbw), tok(sbw), tok(sbw), tok(sbw), tok(kvlora), tok(rope), tok(kcw), tok(2 * kvlora),
                 pl.BlockSpec((1, MLA_HEADS, tm, kcw), lambda b, i: (b, 0, i, 0)))
    in_specs = [tok(D), tab, tab] + [_const_spec(a.shape) for a in (ln1, wmain, qn, kvn, wqn, wuk, wqpe, wqpes)]
    kern = functools.partial(_proj_kernel, sb_scale=sb_scale, mla_scale=mla_scale, sbw=sbw, qlora=qlora, kvlora=kvlora,
                             rope=rope)
    return pl.pallas_call(kern, out_shape=out_shape, grid=grid, in_specs=in_specs, out_specs=out_specs,
                          compiler_params=_params(("parallel", "parallel")), name="proj")(
        x, cos_t, sin_t, ln1, wmain, qn, kvn, wqn, wuk, wqpe, wqpes)


def _head_pair_blockdiag(blk):
    lane = lax.broadcasted_iota(jnp.int32, blk.shape, 1)
    zero = jnp.zeros_like(blk)
    half = LANES // 2
    return jnp.concatenate([jnp.where(lane < half, blk, zero), jnp.where(lane >= half, blk, zero)], axis=0)


def _sb_step(qs, kblk, vblk, tri, carries, accs, keep):
    tq = qs[0].shape[0]
    lbs, loms, parts = [], [], []
    for p, q in enumerate(qs):
        z = _dot_nt(q, _head_pair_blockdiag(kblk[:, p * LANES:(p + 1) * LANES]))
        lb = jnp.minimum(z, 0.0) - jnp.log(1.0 + jnp.exp(-jnp.abs(z)))
        lom = lb - z
        if keep is not None:
            lom = jnp.where(keep, lom, 0.0)
        hi = lom.astype(BF16)
        lbs.append(lb)
        loms.append(lom)
        parts += [hi, (lom - hi.astype(F32)).astype(BF16)]
    r = _dot(jnp.concatenate(parts, axis=0), tri)
    new_c, new_a = [], []
    for p in range(len(qs)):
        incl = r[2 * p * tq:(2 * p + 1) * tq] + r[(2 * p + 1) * tq:(2 * p + 2) * tq]
        w = jnp.exp(lbs[p] + (incl - loms[p]) + carries[p])
        if keep is not None:
            w = jnp.where(keep, w, 0.0)
        new_a.append(accs[p] + _dot(w.astype(BF16), _head_pair_blockdiag(vblk[:, p * LANES:(p + 1) * LANES])))
        tot = jnp.concatenate([jnp.broadcast_to(incl[:, 0:1], (tq, SB_KBLK)),
                               jnp.broadcast_to(incl[:, SB_KBLK:SB_KBLK + 1], (tq, SB_KBLK))], axis=1)
        new_c.append(carries[p] + tot)
    return new_c, new_a


def _sb_prompt_kernel(q_ref, k_ref, v_ref, tri_ref, o_ref, carry_ref, acc_ref, *, tq):
    i = pl.program_id(1)
    npair = q_ref.shape[2] // LANES
    qs = [q_ref[0, :, p * LANES:(p + 1) * LANES] for p in range(npair)]
    tri = tri_ref[...]
    row = lax.broadcasted_iota(jnp.int32, (tq, 2 * SB_KBLK), 0)
    col = lax.broadcasted_iota(jnp.int32, (tq, 2 * SB_KBLK), 1)
    ks = pl.multiple_of(i * SB_KBLK, SB_KBLK)
    zeros_c = [jnp.zeros((tq, 2 * SB_KBLK), F32)] * npair
    zeros_a = [jnp.zeros((tq, LANES), F32)] * npair
    c, a = _sb_step(qs, k_ref[0, pl.ds(ks, SB_KBLK), :], v_ref[0, pl.ds(ks, SB_KBLK), :], tri, zeros_c, zeros_a,
                    (col & (SB_KBLK - 1)) < row)
    for p in range(npair):
        carry_ref[p] = c[p]
        acc_ref[p] = a[p]

    def cond(s):
        j, live = s
        return jnp.logical_and(j >= 0, live > SB_DEAD_CARRY)

    def body(s):
        j, _ = s
        ks = pl.multiple_of(j * SB_KBLK, SB_KBLK)
        c, a = _sb_step(qs, k_ref[0, pl.ds(ks, SB_KBLK), :], v_ref[0, pl.ds(ks, SB_KBLK), :], tri,
                        [carry_ref[p] for p in range(npair)], [acc_ref[p] for p in range(npair)], None)
        for p in range(npair):
            carry_ref[p] = c[p]
            acc_ref[p] = a[p]
        return j - 1, jnp.max(functools.reduce(jnp.maximum, c))

    lax.while_loop(cond, body, (i - 1, jnp.max(functools.reduce(jnp.maximum, c))))
    for p in range(npair):
        o_ref[0, :, p * LANES:(p + 1) * LANES] = acc_ref[p].astype(o_ref.dtype)


def _sb_prompt(q, kb, vb, tri):
    B, T, W = q.shape
    tq = SB_KBLK
    npair = W // LANES
    qspec = pl.BlockSpec((1, tq, W), lambda b, i: (b, i, 0))
    kspec = pl.BlockSpec((1, T, W), lambda b, i: (b, 0, 0))
    return pl.pallas_call(
        functools.partial(_sb_prompt_kernel, tq=tq), out_shape=jax.ShapeDtypeStruct((B, T, W), BF16), grid=(B, T // tq),
        in_specs=[qspec, kspec, kspec, _const_spec(tri.shape)], out_specs=qspec,
        scratch_shapes=[pltpu.VMEM((npair, tq, 2 * SB_KBLK), F32), pltpu.VMEM((npair, tq, LANES), F32)],
        compiler_params=_params(("parallel", "arbitrary")), name="sb_prompt")(q, kb, vb, tri)


def _sb_sample_kernel(q_ref, k_ref, v_ref, tri_ref, o_ref, *, tq, nkb, qpos0):
    npair = q_ref.shape[2] // LANES
    qs = [q_ref[0, :, p * LANES:(p + 1) * LANES] for p in range(npair)]
    tri = tri_ref[...]
    row = lax.broadcasted_iota(jnp.int32, (tq, 2 * SB_KBLK), 0)
    col = lax.broadcasted_iota(jnp.int32, (tq, 2 * SB_KBLK), 1)
    c = [jnp.zeros((tq, 2 * SB_KBLK), F32)] * npair
    a = [jnp.zeros((tq, LANES), F32)] * npair
    for j in range(nkb - 1, -1, -1):
        ks = j * SB_KBLK
        c, a = _sb_step(qs, k_ref[0, ks:ks + SB_KBLK, :], v_ref[0, ks:ks + SB_KBLK, :], tri, c, a,
                        (ks + (col & (SB_KBLK - 1))) < (qpos0 + row))
    for p in range(npair):
        o_ref[0, :, p * LANES:(p + 1) * LANES] = a[p].astype(o_ref.dtype)


def _sb_sample(q, k_all, v_all, tri, *, qpos0):
    B, tq, W = q.shape
    Tk = k_all.shape[1]
    qspec = pl.BlockSpec((1, tq, W), lambda b: (b, 0, 0))
    kspec = pl.BlockSpec((1, Tk, W), lambda b: (b, 0, 0))
    return pl.pallas_call(
        functools.partial(_sb_sample_kernel, tq=tq, nkb=Tk // SB_KBLK, qpos0=qpos0),
        out_shape=jax.ShapeDtypeStruct((B, tq, W), BF16), grid=(B,),
        in_specs=[qspec, kspec, kspec, _const_spec(tri.shape)], out_specs=qspec,
        compiler_params=_params(("parallel",)), name="sb_sample")(q, k_all, v_all, tri)


def _mla_prompt_kernel(q_ref, kt_ref, vx_ref, o_ref, s_ref, p_ref, m_ref, a_ref, acc_ref, *, tq, tk, kvlora):
    i = pl.program_id(1)
    H = q_ref.shape[1]
    kc = q_ref.shape[3]
    group_heads = H // MLA_ROW_GROUPS
    gr = group_heads * tq
    m_ref[...] = jnp.full(m_ref.shape, -jnp.inf, F32)
    acc_ref[...] = jnp.zeros(acc_ref.shape, F32)

    def step(j, masked):
        ks = pl.multiple_of(j * tk, tk)
        kt = kt_ref[0, :, pl.ds(ks, tk)]
        vx = vx_ref[0, pl.ds(ks, tk), :]
        for g in range(MLA_ROW_GROUPS):
            qg = q_ref[0, g * group_heads:(g + 1) * group_heads].reshape(gr, kc)
            s_ref[g * gr:(g + 1) * gr, :] = _dot(qg, kt)
        def score_tiles(r0):
            rs = slice(r0, r0 + MLA_ROW_CHUNK)
            tiles = [s_ref[rs, t * LANES:(t + 1) * LANES] for t in range(tk // LANES)]
            if masked:
                row = lax.broadcasted_iota(jnp.int32, (MLA_ROW_CHUNK, LANES), 0)
                col = lax.broadcasted_iota(jnp.int32, (MLA_ROW_CHUNK, LANES), 1)
                qchunk = (i * tq + (r0 % tq) + row) >> CHUNK_SHIFT
                tiles = [jnp.where(((j * tk + t * LANES + col) >> CHUNK_SHIFT) <= qchunk, st, -jnp.inf)
                         for t, st in enumerate(tiles)]
            return tiles

        for g in range(MLA_ROW_GROUPS):
            for c in range(gr // MLA_ROW_CHUNK):
                r0 = g * gr + c * MLA_ROW_CHUNK
                rs = slice(r0, r0 + MLA_ROW_CHUNK)
                tiles = score_tiles(r0)
                m_old = m_ref[rs, :]
                m_new = jnp.maximum(m_old, jnp.max(functools.reduce(jnp.maximum, tiles), axis=-1, keepdims=True))
                for t, st in enumerate(tiles):
                    p_ref[rs, t * LANES:(t + 1) * LANES] = jnp.exp2(st - m_new).astype(BF16)
                a_ref[rs, :] = jnp.exp2(m_old - m_new)
                m_ref[rs, :] = m_new
            gs = slice(g * gr, (g + 1) * gr)
            alpha = a_ref[gs, :]
            acc_ref[gs, :] = jnp.concatenate([alpha, alpha], axis=-1) * acc_ref[gs, :] + _dot(p_ref[gs, :], vx)

    nfull = (i * tq) // tk

    def body(j, c):
        step(j, False)
        return c

    lax.fori_loop(0, nfull, body, 0)
    for d in range(tq // tk):
        step(nfull + d, True)
    for h in range(H):
        hs = slice(h * tq, (h + 1) * tq)
        o_ref[0, :, h * kvlora:(h + 1) * kvlora] = (acc_ref[hs, :kvlora] / acc_ref[hs, kvlora:]).astype(o_ref.dtype)


def _mla_prompt(qcat, kcat_t, vext, *, tq, tk, kvlora):
    B, H, T, KC = qcat.shape
    grid = (B, T // tq)
    rows = H * tq
    return pl.pallas_call(
        functools.partial(_mla_prompt_kernel, tq=tq, tk=tk, kvlora=kvlora),
        out_shape=jax.ShapeDtypeStruct((B, T, H * kvlora), BF16), grid=grid,
        in_specs=[pl.BlockSpec((1, H, tq, KC), lambda b, i: (b, 0, i, 0)), pl.BlockSpec((1, KC, T), lambda b, i: (b, 0, 0)),
                  pl.BlockSpec((1, T, 2 * kvlora), lambda b, i: (b, 0, 0))],
        out_specs=pl.BlockSpec((1, tq, H * kvlora), lambda b, i: (b, i, 0)),
        scratch_shapes=[pltpu.VMEM((rows, tk), F32), pltpu.VMEM((rows, tk), BF16), pltpu.VMEM((rows, LANES), F32),
                        pltpu.VMEM((rows, LANES), F32), pltpu.VMEM((rows, 2 * kvlora), F32)],
        compiler_params=_params(("parallel", "arbitrary")), name="mla_prompt")(qcat, kcat_t, vext)


def _mla_sample_kernel(q_ref, kc_ref, o_ref, *, tq, qpos0, nvalid, kvlora):
    H = q_ref.shape[1]
    q = q_ref[0].reshape(H * tq, q_ref.shape[3])
    kall = kc_ref[0]
    s = _dot_nt(q, kall)
    row = lax.broadcasted_iota(jnp.int32, s.shape, 0)
    col = lax.broadcasted_iota(jnp.int32, s.shape, 1)
    qpos = qpos0 + (row & (tq - 1))
    keep = jnp.logical_and((col >> CHUNK_SHIFT) <= (qpos >> CHUNK_SHIFT), col < nvalid)
    s = jnp.where(keep, s, -jnp.inf)
    p = jnp.exp2(s - jnp.max(s, axis=-1, keepdims=True))
    out = _dot(p.astype(BF16), kall[:, :kvlora]) / jnp.sum(p, axis=-1, keepdims=True)
    for h in range(H):
        o_ref[0, :, h * kvlora:(h + 1) * kvlora] = out[h * tq:(h + 1) * tq].astype(o_ref.dtype)


def _mla_sample(qcat, kcat_all, *, qpos0, nvalid, kvlora):
    B, H, tq, KC = qcat.shape
    Tk = kcat_all.shape[1]
    return pl.pallas_call(
        functools.partial(_mla_sample_kernel, tq=tq, qpos0=qpos0, nvalid=nvalid, kvlora=kvlora),
        out_shape=jax.ShapeDtypeStruct((B, tq, H * kvlora), BF16), grid=(B,),
        in_specs=[pl.BlockSpec((1, H, tq, KC), lambda b: (b, 0, 0, 0)), pl.BlockSpec((1, Tk, KC), lambda b: (b, 0, 0))],
        out_specs=pl.BlockSpec((1, tq, H * kvlora), lambda b: (b, 0, 0)),
        compiler_params=_params(("parallel",)), name="mla_sample")(qcat, kcat_all)


def _router_logits(xf_bf, wrh_ref, wrl_ref, br_ref):
    return _dot(xf_bf, wrh_ref[...]) + _dot(xf_bf, wrl_ref[...]) + br_ref[...]


def _route(logits, gidx=None):
    lane = lax.broadcasted_iota(jnp.int32, logits.shape, 1)
    ninf = -jnp.inf
    gl = jnp.where(lane < N_GROUPS, logits, ninf)
    gmax = jnp.max(gl, axis=-1, keepdims=True)
    gexp = jnp.where(lane < N_GROUPS, jnp.exp(logits - gmax), 0.0)
    if gidx is None:
        gidx = jnp.min(jnp.where(gl == gmax, lane, LANES), axis=-1, keepdims=True)
    g_val = jnp.sum(jnp.where(lane == gidx, gexp, 0.0), axis=-1, keepdims=True) / jnp.sum(gexp, axis=-1, keepdims=True)
    lo = ROUTER_LANE0 + EXPERTS_PER_GROUP * gidx
    el = jnp.where(jnp.logical_and(lane >= lo, lane < lo + EXPERTS_PER_GROUP), logits, ninf)
    v1 = jnp.max(el, axis=-1, keepdims=True)
    i1 = jnp.min(jnp.where(el == v1, lane, LANES), axis=-1, keepdims=True)
    el2 = jnp.where(lane == i1, ninf, el)
    v2 = jnp.max(el2, axis=-1, keepdims=True)
    i2 = jnp.min(jnp.where(el2 == v2, lane, LANES), axis=-1, keepdims=True)
    e2 = jnp.exp(v2 - v1)
    den = 1.0 + e2
    return jnp.where(lane == i1, (1.0 / den) * g_val, 0.0) + jnp.where(lane == i2, (e2 / den) * g_val, 0.0), gidx


def _merge_kernel(x_ref, sb_ref, lat_ref, ln1_ref, wg_ref, wuv_ref, wsb_ref, wmla_ref, wout_ref, ln2_ref, wrh_ref, wrl_ref,
                  br_ref, tril_ref, h_ref, xf_ref, xrows_ref, comb_ref, info_ref, cnt_ref, *, d):
    @pl.when(pl.program_id(0) == 0)
    def _():
        cnt_ref[...] = jnp.zeros(cnt_ref.shape, F32)

    x = x_ref[...]
    xn = _rms(x, ln1_ref[...]).astype(BF16)
    g = _dot(xn, wg_ref[...])
    mla_o = _dot(lat_ref[...], wuv_ref[...]).astype(BF16)
    merged = jax.nn.sigmoid(g[:, :d]) * _dot(sb_ref[...], wsb_ref[...]) + jax.nn.sigmoid(g[:, d:]) * _dot(mla_o, wmla_ref[...])
    h = x + _dot(merged.astype(BF16), wout_ref[...])
    h_ref[...] = h
    xf = _rms(h, ln2_ref[...]).astype(BF16)
    xf_ref[...] = xf
    _to_row_tiles(xrows_ref, xf.astype(F32))
    comb, gidx = _route(_router_logits(xf, wrh_ref, wrl_ref, br_ref))
    comb_ref[...] = comb
    lane = lax.broadcasted_iota(jnp.int32, comb.shape, 1)
    onehot = lane == gidx
    before = _dot(tril_ref[...], jnp.where(onehot, 1.0, 0.0).astype(BF16)) + cnt_ref[...]
    rank = jnp.sum(jnp.where(onehot, before, 0.0), axis=-1, keepdims=True).astype(jnp.int32)
    info_ref[...] = jnp.where(lane == 0, rank, jnp.where(lane == 1, gidx, 0))
    cnt_ref[...] += jnp.sum(jnp.where(onehot, 1.0, 0.0), axis=0, keepdims=True)


def _merge(x, sb_o, lat, ln1, wg, wuv, wsb, wmla, wout, ln2, wrh, wrl, br, tril, *, tm):
    N, D = x.shape
    tok = lambda w: pl.BlockSpec((tm, w), lambda i: (i, 0))
    consts = (ln1, wg, wuv, wsb, wmla, wout, ln2, wrh, wrl, br, tril)
    row_tiles = (D // LANES, LANES)
    return pl.pallas_call(
        functools.partial(_merge_kernel, d=D),
        out_shape=(jax.ShapeDtypeStruct((N, D), F32), jax.ShapeDtypeStruct((N, D), BF16),
                   jax.ShapeDtypeStruct((N,) + row_tiles, F32), jax.ShapeDtypeStruct((N, LANES), F32),
                   jax.ShapeDtypeStruct((N, LANES), jnp.int32), jax.ShapeDtypeStruct((1, LANES), F32)),
        grid=(N // tm,), in_specs=[tok(D), tok(sb_o.shape[1]), tok(lat.shape[1])] + [_const_spec(a.shape) for a in consts],
        out_specs=(tok(D), tok(D), pl.BlockSpec((tm,) + row_tiles, lambda i: (i, 0, 0)), tok(LANES), tok(LANES),
                   _const_spec((1, LANES))),
        compiler_params=_params(("arbitrary",)), name="merge")(x, sb_o, lat, *consts)


def _moe_kernel(xf_ref, comb_ref, h_ref, wg_ref, wu_ref, wd_ref, lnf_ref, y_ref, acc_ref, *, final_norm):
    e = pl.program_id(1)

    @pl.when(e == 0)
    def _():
        acc_ref[...] = jnp.zeros(acc_ref.shape, F32)

    xf = xf_ref[...]
    comb = comb_ref[...]
    lane = lax.broadcasted_iota(jnp.int32, comb.shape, 1)
    c = jnp.sum(jnp.where(lane == e + ROUTER_LANE0, comb, 0.0), axis=-1, keepdims=True)
    act = jax.nn.silu(_dot(xf, wg_ref[0])) * _dot(xf, wu_ref[0])
    acc_ref[...] += _dot((act * c).astype(BF16), wd_ref[0])

    @pl.when(e == pl.num_programs(1) - 1)
    def _():
        y = h_ref[...] + acc_ref[...]
        y_ref[...] = _rms(y, lnf_ref[...]) if final_norm else y


def _moe(xf, comb, h, wg, wu, wd, lnf, *, tm, final_norm):
    N, D = h.shape
    E, _, F = wg.shape
    tok = lambda w: pl.BlockSpec((tm, w), lambda i, e: (i, 0))
    return pl.pallas_call(
        functools.partial(_moe_kernel, final_norm=final_norm), out_shape=jax.ShapeDtypeStruct((N, D), F32), grid=(N // tm, E),
        in_specs=[tok(D), tok(LANES), tok(D), pl.BlockSpec((1, D, F), lambda i, e: (e, 0, 0)),
                  pl.BlockSpec((1, D, F), lambda i, e: (e, 0, 0)), pl.BlockSpec((1, F, D), lambda i, e: (e, 0, 0)),
                  pl.BlockSpec((1, D), lambda i, e: (0, 0))],
        out_specs=tok(D), scratch_shapes=[pltpu.VMEM((tm, D), F32)],
        compiler_params=_params(("parallel", "arbitrary")), name="moe")(xf, comb, h, wg, wu, wd, lnf)


def _to_row_tiles(ref, x):
    for k in range(ref.shape[1]):
        ref[:, k, :] = x[:, k * LANES:(k + 1) * LANES]


def _from_row_tiles(ref):
    return jnp.concatenate([ref[:, k, :] for k in range(ref.shape[1])], axis=-1)


def _row_dma_loop(n, make_copy):
    unroll = 8

    def issue(k, c):
        for u in range(unroll):
            make_copy(k * unroll + u).start(priority=u % 2)
        return c

    def drain(r, c):
        make_copy(0).wait()
        return c

    lax.fori_loop(0, n // unroll, issue, 0)
    lax.fori_loop(0, n, drain, 0, unroll=unroll)


def _dispatch_kernel(pos_hbm, x_hbm, init_hbm, out_hbm, idx_ref, sem_idx, sem_rows, *, tm):
    del init_hbm
    t0 = pl.program_id(0) * tm
    idx_copy = pltpu.make_async_copy(pos_hbm.at[pl.program_id(0)], idx_ref, sem_idx)
    idx_copy.start()
    idx_copy.wait()
    _row_dma_loop(tm, lambda r: pltpu.make_async_copy(x_hbm.at[t0 + r], out_hbm.at[idx_ref[0, r]], sem_rows))


def _dispatch(pos, x_rows, n_rows, *, tm):
    N = x_rows.shape[0]
    init = jnp.zeros((n_rows,) + x_rows.shape[1:], x_rows.dtype)
    anyspec = pl.BlockSpec(memory_space=pl.ANY)
    return pl.pallas_call(
        functools.partial(_dispatch_kernel, tm=tm), out_shape=jax.ShapeDtypeStruct(init.shape, init.dtype), grid=(N // tm,),
        in_specs=[anyspec, anyspec, anyspec], out_specs=anyspec,
        scratch_shapes=[pltpu.SMEM((1, tm), jnp.int32), pltpu.SemaphoreType.DMA, pltpu.SemaphoreType.DMA],
        input_output_aliases={2: 0}, compiler_params=_params(("arbitrary",)), name="moe_dispatch")(
        pos.reshape(N // tm, 1, tm), x_rows, init)


def _moe_grouped_kernel(grp_ref, valid_ref, xs_ref, wg_ref, wu_ref, wd_ref, wrh_ref, wrl_ref, br_ref, ys_ref):
    w = pl.program_id(0)
    grp = grp_ref[w]

    @pl.when(valid_ref[w] == 0)
    def _():
        ys_ref[...] = jnp.zeros(ys_ref.shape, F32)

    @pl.when(valid_ref[w] == 1)
    def _():
        x = _from_row_tiles(xs_ref).astype(BF16)
        comb = _route(_router_logits(x, wrh_ref, wrl_ref, br_ref), grp)[0]
        lane = lax.broadcasted_iota(jnp.int32, comb.shape, 1)
        lane0 = ROUTER_LANE0 + EXPERTS_PER_GROUP * grp
        out = None
        for e in range(EXPERTS_PER_GROUP):
            c = jnp.sum(jnp.where(lane == lane0 + e, comb, 0.0), axis=-1, keepdims=True)
            act = jax.nn.silu(_dot(x, wg_ref[e])) * _dot(x, wu_ref[e])
            y = _dot((act * c).astype(BF16), wd_ref[e])
            out = y if out is None else out + y
        _to_row_tiles(ys_ref, out)


def _moe_grouped(grp, valid, xs, wg, wu, wd, wrh, wrl, br, *, tm):
    S, tiles, _ = xs.shape
    D = tiles * LANES
    F = wg.shape[2]
    rows = pl.BlockSpec((tm, tiles, LANES), lambda w, grp, valid: (w, 0, 0))
    group = lambda w, grp, valid: (grp[w], 0, 0)
    const = lambda shape: pl.BlockSpec(shape, lambda w, grp, valid: (0,) * len(shape))
    grid_spec = pltpu.PrefetchScalarGridSpec(
        num_scalar_prefetch=2, grid=(S // tm,),
        in_specs=[rows, pl.BlockSpec((EXPERTS_PER_GROUP, D, F), group), pl.BlockSpec((EXPERTS_PER_GROUP, D, F), group),
                  pl.BlockSpec((EXPERTS_PER_GROUP, F, D), group), const(wrh.shape), const(wrl.shape), const(br.shape)],
        out_specs=rows)
    return pl.pallas_call(_moe_grouped_kernel, out_shape=jax.ShapeDtypeStruct(xs.shape, F32), grid_spec=grid_spec,
                          compiler_params=_params(("arbitrary",)), name="moe_grouped")(
        grp, valid, xs, wg, wu, wd, wrh, wrl, br)


def _combine_kernel(pos_hbm, h_ref, lnf_ref, ys_hbm, y_ref, idx_ref, buf_ref, sem_idx, sem_rows, *, tm, final_norm):
    idx_copy = pltpu.make_async_copy(pos_hbm.at[pl.program_id(0)], idx_ref, sem_idx)
    idx_copy.start()
    idx_copy.wait()
    _row_dma_loop(tm, lambda r: pltpu.make_async_copy(ys_hbm.at[idx_ref[0, r]], buf_ref.at[r], sem_rows))
    y = h_ref[...] + _from_row_tiles(buf_ref)
    y_ref[...] = _rms(y, lnf_ref[...]) if final_norm else y


def _combine(pos, h, lnf, ys, *, tm, final_norm):
    N, D = h.shape
    tiles = D // LANES
    anyspec = pl.BlockSpec(memory_space=pl.ANY)
    return pl.pallas_call(
        functools.partial(_combine_kernel, tm=tm, final_norm=final_norm), out_shape=jax.ShapeDtypeStruct((N, D), F32),
        grid=(N // tm,), in_specs=[anyspec, pl.BlockSpec((tm, D), lambda i: (i, 0)), _const_spec(lnf.shape), anyspec],
        out_specs=pl.BlockSpec((tm, D), lambda i: (i, 0)),
        scratch_shapes=[pltpu.SMEM((1, tm), jnp.int32), pltpu.VMEM((tm, tiles, LANES), F32), pltpu.SemaphoreType.DMA,
                        pltpu.SemaphoreType.DMA],
        compiler_params=_params(("arbitrary",)), name="moe_combine")(pos.reshape(N // tm, 1, tm), h, lnf, ys)


def _group_layout(info, cnt, *, tm, n_tiles):
    rank, gidx = info[:, 0], info[:, 1]
    counts = cnt[0, :N_GROUPS].astype(jnp.int32)
    ntiles = (counts + tm - 1) // tm
    tile_end = jnp.cumsum(ntiles)
    pos = (tile_end - ntiles)[gidx] * tm + rank
    tile = jnp.arange(n_tiles, dtype=jnp.int32)
    valid = (tile < tile_end[-1]).astype(jnp.int32)
    grp = jnp.sum((jnp.minimum(tile, tile_end[-1] - 1)[:, None] >= tile_end[None, :]).astype(jnp.int32), axis=1)
    return pos, grp, valid


def _pad_lanes(w):
    return jnp.pad(w, [(0, 0)] * (w.ndim - 1) + [(0, LANES - w.shape[-1])])


def _swap_halves(w):
    half = w.shape[-1] // 2
    return jnp.concatenate([w[..., half:], w[..., :half]], axis=-1)


def _rope_tables(pos, rope):
    half = rope // 2
    inv_freq = ROPE_BASE ** (-jnp.arange(half, dtype=F32) / half)
    ang = pos.astype(F32)[:, None] * inv_freq[None, :]
    cos, sin = jnp.cos(ang), jnp.sin(ang)
    return _pad_lanes(jnp.concatenate([cos, cos], axis=-1)), _pad_lanes(jnp.concatenate([-sin, sin], axis=-1))


def _sb_scan_operator():
    n = SB_KBLK
    tri = (jnp.arange(n)[:, None] >= jnp.arange(n)[None, :]).astype(F32)
    z = jnp.zeros((n, n), F32)
    return jnp.concatenate([jnp.concatenate([tri, z], axis=1), jnp.concatenate([z, tri], axis=1)], axis=0).astype(BF16)


def _strict_lower(n):
    return (jnp.arange(n)[:, None] > jnp.arange(n)[None, :]).astype(BF16)


def _pad_rows(a, rows):
    return jnp.pad(a, [(0, 0), (0, rows - a.shape[1]), (0, 0)])


def kernel(x_prompt, x_sample, cache_sb_k, cache_sb_v, cache_mla_ckv, cache_mla_kpe, ln1, w_in, q_norm, w_q_up, kv_norm, w_uk,
           w_uv, w_sb_branch, w_mla_branch, w_out, ln2, w_group, b_group, w_router, b_router, w_gate, w_up, w_down, ln_f):
    B, T, D = x_prompt.shape
    SBt, SQ, _ = x_sample.shape
    depth = ln1.shape[0]
    past = cache_sb_k.shape[2]
    sbw = cache_sb_k.shape[3] * cache_sb_k.shape[4]
    sb_hd = cache_sb_k.shape[4]
    qlora = q_norm.shape[1]
    kvlora = kv_norm.shape[1]
    rope = cache_mla_kpe.shape[3]
    nope = w_uk.shape[3]
    vdim = w_uv.shape[3]
    H = w_uk.shape[2]
    sb_scale = 1.0 / math.sqrt(sb_hd)
    mla_scale = LOG2E / math.sqrt(nope + rope)

    cos_p, sin_p = _rope_tables(jnp.arange(T, dtype=jnp.int32), rope)
    pos_s = past + jnp.arange(SQ, dtype=jnp.int32)
    cos_s, sin_s = _rope_tables(jnp.tile(pos_s, SBt), rope)
    tri = _sb_scan_operator()
    eye = jnp.eye(H, dtype=F32)
    row = lambda a: a.reshape(1, -1)

    tm_p = min(512, T)
    tq_mla = min(512, T)
    ns = SBt * SQ
    tk_all = -(-(past + SQ) // SB_KBLK) * SB_KBLK

    xp = x_prompt
    xs = x_sample.reshape(1, ns, D)
    outs_p = [[], [], [], []]
    outs_s = [[], [], [], []]
    for l in range(depth):
        wl = w_in[l]
        c = 3 * sbw + qlora + kvlora
        wkpe = wl[:, c:c + rope]
        wmain = jnp.concatenate([wl[:, :c], _pad_lanes(wkpe), _pad_lanes(_swap_halves(wkpe))], axis=1).astype(BF16)
        wgates = wl[:, c + rope:].astype(BF16)
        wq = w_q_up[l]
        wqn = wq[:, :, :nope].reshape(qlora, H * nope).astype(BF16)
        wqpe = _pad_lanes(wq[:, :, nope:]).reshape(qlora, H * LANES).astype(BF16)
        wqpes = _pad_lanes(_swap_halves(wq[:, :, nope:])).reshape(qlora, H * LANES).astype(BF16)
        wuk_bd = jnp.einsum('hdc,hg->hdgc', jnp.transpose(w_uk[l], (1, 2, 0)), eye).reshape(H * nope, H * kvlora).astype(BF16)
        wuv_bd = jnp.einsum('hcd,hg->hcgd', jnp.transpose(w_uv[l], (1, 0, 2)), eye).reshape(H * kvlora, H * vdim).astype(BF16)
        wr = _pad_lanes(jnp.concatenate([w_group[l], w_router[l]], axis=1))
        wrh = wr.astype(BF16)
        wrl = (wr - wrh.astype(F32)).astype(BF16)
        br = _pad_lanes(jnp.concatenate([b_group[l], b_router[l]]).reshape(1, -1))
        wsb = w_sb_branch[l].astype(BF16)
        wmla = w_mla_branch[l].astype(BF16)
        wo = w_out[l].astype(BF16)
        wg_e = w_gate[l].astype(BF16)
        wu_e = w_up[l].astype(BF16)
        wd_e = w_down[l].astype(BF16)
        lnf = row(ln_f)
        last = l == depth - 1
        proj_args = (row(ln1[l]), wmain, row(q_norm[l]), row(kv_norm[l]), wqn, wuk_bd, wqpe, wqpes)
        proj_kw = dict(sb_scale=sb_scale, mla_scale=mla_scale, sbw=sbw, qlora=qlora, kvlora=kvlora, rope=rope)
        merge_args = (row(ln1[l]), wgates, wuv_bd, wsb, wmla, wo, row(ln2[l]), wrh, wrl, br)

        def moe_block(h, xf, xf_rows, comb, info, cnt):
            n = h.shape[0]
            if n < MOE_GROUPED_MIN_TOKENS:
                return _moe(xf, comb, h, wg_e, wu_e, wd_e, lnf, tm=n, final_norm=last)
            tm_g = min(MOE_GROUP_TILE, n // N_GROUPS)
            tm_tok = min(MOE_ROW_DMA_TILE, n)
            n_tiles = n // tm_g + N_GROUPS
            pos, grp, valid = _group_layout(info, cnt, tm=tm_g, n_tiles=n_tiles)
            xs_rows = _dispatch(pos, xf_rows, n_tiles * tm_g, tm=tm_tok)
            ys_rows = _moe_grouped(grp, valid, xs_rows, wg_e, wu_e, wd_e, wrh, wrl, br, tm=tm_g)
            return _combine(pos, h, lnf, ys_rows, tm=tm_tok, final_norm=last)

        q, k, v, kb, vb, ckv, kpe, kcat, vext, qcat = _proj(xp, cos_p, sin_p, *proj_args, tm=tm_p, **proj_kw)
        sb_o = _sb_prompt(q, kb, vb, tri)
        lat = _mla_prompt(qcat, jnp.swapaxes(kcat, 1, 2), vext, tq=tq_mla, tk=tq_mla, kvlora=kvlora)
        y = moe_block(*_merge(xp.reshape(B * T, D), sb_o.reshape(B * T, sbw), lat.reshape(B * T, H * kvlora), *merge_args,
                              _strict_lower(tm_p), tm=tm_p))
        xp = y.reshape(B, T, D)
        for lst, a in zip(outs_p, (k, v, ckv, kpe)):
            lst.append(a)

        q2, k2, v2, kb2, vb2, ckv2, kpe2, kcat2, _, qcat2 = _proj(xs, cos_s, sin_s, *proj_args, tm=ns, **proj_kw)
        per_stream = lambda a: a.reshape(SBt, SQ, a.shape[-1])
        k_all = _pad_rows(jnp.concatenate([cache_sb_k[l].reshape(SBt, past, sbw).astype(BF16), per_stream(kb2)], axis=1), tk_all)
        v_all = _pad_rows(jnp.concatenate([cache_sb_v[l].reshape(SBt, past, sbw).astype(BF16), per_stream(vb2)], axis=1), tk_all)
        kc_past = jnp.concatenate([cache_mla_ckv[l], _pad_lanes(cache_mla_kpe[l])], axis=-1).astype(BF16)
        kcat_all = _pad_rows(jnp.concatenate([kc_past, per_stream(kcat2)], axis=1), tk_all)
        sb_o2 = _sb_sample(per_stream(q2), k_all, v_all, tri, qpos0=past)
        qcat_s = jnp.transpose(qcat2.reshape(H, SBt, SQ, kvlora + LANES), (1, 0, 2, 3))
        lat2 = _mla_sample(qcat_s, kcat_all, qpos0=past, nvalid=past + SQ, kvlora=kvlora)
        y2 = moe_block(*_merge(xs.reshape(ns, D), sb_o2.reshape(ns, sbw), lat2.reshape(ns, H * kvlora), *merge_args,
                              _strict_lower(ns), tm=ns))
        xs = y2.reshape(1, ns, D)
        for lst, a in zip(outs_s, (k2, v2, ckv2, kpe2)):
            lst.append(a)

    heads = lambda a, n: a.reshape(n, -1, SB_HEADS, sb_hd)
    return (xp, xs.reshape(SBt, SQ, D),
            jnp.stack([heads(a, B) for a in outs_p[0]]), jnp.stack([heads(a, B) for a in outs_p[1]]),
            jnp.stack(outs_p[2]), jnp.stack(outs_p[3]),
            jnp.stack([heads(a.reshape(SBt, SQ, sbw), SBt) for a in outs_s[0]]),
            jnp.stack([heads(a.reshape(SBt, SQ, sbw), SBt) for a in outs_s[1]]),
            jnp.stack([a.reshape(SBt, SQ, kvlora) for a in outs_s[2]]),
            jnp.stack([a.reshape(SBt, SQ, rope) for a in outs_s[3]]))
```

```python
import functools
import math

import jax
import jax.numpy as jnp
from jax import lax
from jax.experimental import pallas as pl
from jax.experimental.pallas import tpu as pltpu

F32 = jnp.float32
BF16 = jnp.bfloat16

CHUNK = 64
CHUNK_SHIFT = CHUNK.bit_length() - 1
assert 1 << CHUNK_SHIFT == CHUNK
SB_HEADS = 8
MLA_HEADS = 8
ROPE_BASE = 10000.0
N_GROUPS = 4
EXPERTS_PER_GROUP = 8
N_EXPERTS = N_GROUPS * EXPERTS_PER_GROUP
RMS_EPS = 1e-6

LANES = 128
SB_KBLK = LANES
SB_DEAD_CARRY = -110.0
MLA_ROW_GROUPS = 8
MLA_ROW_CHUNK = 64
VMEM_LIMIT = 56 * 1024 * 1024
ROUTER_LANE0 = N_GROUPS
MOE_GROUP_TILE = 512
MOE_ROW_DMA_TILE = 2048
MOE_GROUPED_MIN_TOKENS = 2048


def _dot(a, b):
    return jnp.dot(a, b, preferred_element_type=F32)


def _dot_nt(a, b):
    return lax.dot_general(a, b, (((1,), (1,)), ((), ())), preferred_element_type=F32)


def _rms(x, g):
    return x * lax.rsqrt(jnp.mean(x * x, axis=-1, keepdims=True) + RMS_EPS) * g


def _params(sem):
    return pltpu.CompilerParams(dimension_semantics=sem, vmem_limit_bytes=VMEM_LIMIT)


def _const_spec(shape):
    return pl.BlockSpec(shape, lambda *_: (0,) * len(shape))


def _proj_kernel(x_ref, cos_ref, sin_ref, ln1_ref, wmain_ref, qn_ref, kvn_ref, wqn_ref, wuk_ref, wqpe_ref, wqpes_ref,
                 q_ref, k_ref, v_ref, kb_ref, vb_ref, ckv_ref, kpe_ref, kcat_ref, vext_ref, qcat_ref, *, sb_scale, mla_scale, sbw,
                 qlora, kvlora, rope):
    xn = _rms(x_ref[0], ln1_ref[...]).astype(BF16)
    p = _dot(xn, wmain_ref[...])
    o = 0
    q = p[:, o:o + sbw]; o += sbw
    k = p[:, o:o + sbw]; o += sbw
    v = p[:, o:o + sbw]; o += sbw
    cq = p[:, o:o + qlora]; o += qlora
    ckv = p[:, o:o + kvlora]; o += kvlora
    kpe = p[:, o:o + LANES]; o += LANES
    kpe_sw = p[:, o:o + LANES]
    q_ref[0] = (q * sb_scale).astype(BF16)
    k_ref[0] = k
    v_ref[0] = v
    kb_ref[0] = k.astype(BF16)
    vb_ref[0] = v.astype(BF16)
    cos = cos_ref[...]
    sin = sin_ref[...]
    ckvn = _rms(ckv, kvn_ref[...])
    ckv_ref[0] = ckvn
    kpe_rot = kpe * cos + kpe_sw * sin
    kpe_ref[0] = kpe_rot[:, :rope]
    kcat_ref[0] = jnp.concatenate([ckvn, kpe_rot], axis=-1).astype(BF16)
    vext_ref[0] = jnp.concatenate([ckvn, jnp.ones_like(ckvn)], axis=-1).astype(BF16)
    cqn = _rms(cq, qn_ref[...]).astype(BF16)
    qnope = _dot(cqn, wqn_ref[...]).astype(BF16)
    qlat = _dot(qnope, wuk_ref[...])
    qpe = _dot(cqn, wqpe_ref[...])
    qpe_sw = _dot(cqn, wqpes_ref[...])
    for h in range(MLA_HEADS):
        sl = slice(h * LANES, (h + 1) * LANES)
        qcat_ref[0, h, :, :kvlora] = (qlat[:, h * kvlora:(h + 1) * kvlora] * mla_scale).astype(BF16)
        qcat_ref[0, h, :, kvlora:] = ((qpe[:, sl] * cos + qpe_sw[:, sl] * sin) * mla_scale).astype(BF16)


def _proj(x, cos_t, sin_t, ln1, wmain, qn, kvn, wqn, wuk, wqpe, wqpes, *, tm, sb_scale, mla_scale, sbw, qlora, kvlora, rope):
    B, T, D = x.shape
    grid = (B, T // tm)
    tok = lambda w: pl.BlockSpec((1, tm, w), lambda b, i: (b, i, 0))
    tab = pl.BlockSpec((tm, LANES), lambda b, i: (i, 0))
    kcw = kvlora + LANES
    out_shape = (
        jax.ShapeDtypeStruct((B, T, sbw), BF16),
        jax.ShapeDtypeStruct((B, T, sbw), F32),
        jax.ShapeDtypeStruct((B, T, sbw), F32),
        jax.ShapeDtypeStruct((B, T, sbw), BF16),
        jax.ShapeDtypeStruct((B, T, sbw), BF16),
        jax.ShapeDtypeStruct((B, T, kvlora), F32),
        jax.ShapeDtypeStruct((B, T, rope), F32),
        jax.ShapeDtypeStruct((B, T, kcw), BF16),
        jax.ShapeDtypeStruct((B, T, 2 * kvlora), BF16),
        jax.ShapeDtypeStruct((B, MLA_HEADS, T, kcw), BF16),
    )
    out_specs = (tok(sbw), tok(sbw), tok(sbw), tok(sbw), tok(sbw), tok(kvlora), tok(rope), tok(kcw), tok(2 * kvlora),
                 pl.BlockSpec((1, MLA_HEADS, tm, kcw), lambda b, i: (b, 0, i, 0)))
    in_specs = [tok(D), tab, tab] + [_const_spec(a.shape) for a in (ln1, wmain, qn, kvn, wqn, wuk, wqpe, wqpes)]
    kern = functools.partial(_proj_kernel, sb_scale=sb_scale, mla_scale=mla_scale, sbw=sbw, qlora=qlora, kvlora=kvlora,
                             rope=rope)
    return pl.pallas_call(kern, out_shape=out_shape, grid=grid, in_specs=in_specs, out_specs=out_specs,
                          compiler_params=_params(("parallel", "parallel")), name="proj")(
        x, cos_t, sin_t, ln1, wmain, qn, kvn, wqn, wuk, wqpe, wqpes)


def _head_pair_blockdiag(blk):
    lane = lax.broadcasted_iota(jnp.int32, blk.shape, 1)
    zero = jnp.zeros_like(blk)
    half = LANES // 2
    return jnp.concatenate([jnp.where(lane < half, blk, zero), jnp.where(lane >= half, blk, zero)], axis=0)


def _sb_step(qs, kblk, vblk, tri, carries, accs, keep):
    tq = qs[0].shape[0]
    lbs, loms, parts = [], [], []
    for p, q in enumerate(qs):
        z = _dot_nt(q, _head_pair_blockdiag(kblk[:, p * LANES:(p + 1) * LANES]))
        lb = jnp.minimum(z, 0.0) - jnp.log(1.0 + jnp.exp(-jnp.abs(z)))
        lom = lb - z
        if keep is not None:
            lom = jnp.where(keep, lom, 0.0)
        hi = lom.astype(BF16)
        lbs.append(lb)
        loms.append(lom)
        parts += [hi, (lom - hi.astype(F32)).astype(BF16)]
    r = _dot(jnp.concatenate(parts, axis=0), tri)
    new_c, new_a = [], []
    for p in range(len(qs)):
        incl = r[2 * p * tq:(2 * p + 1) * tq] + r[(2 * p + 1) * tq:(2 * p + 2) * tq]
        w = jnp.exp(lbs[p] + (incl - loms[p]) + carries[p])
        if keep is not None:
            w = jnp.where(keep, w, 0.0)
        new_a.append(accs[p] + _dot(w.astype(BF16), _head_pair_blockdiag(vblk[:, p * LANES:(p + 1) * LANES])))
        tot = jnp.concatenate([jnp.broadcast_to(incl[:, 0:1], (tq, SB_KBLK)),
                               jnp.broadcast_to(incl[:, SB_KBLK:SB_KBLK + 1], (tq, SB_KBLK))], axis=1)
        new_c.append(carries[p] + tot)
    return new_c, new_a


def _sb_prompt_kernel(q_ref, k_ref, v_ref, tri_ref, o_ref, carry_ref, acc_ref, *, tq):
    i = pl.program_id(1)
    npair = q_ref.shape[2] // LANES
    qs = [q_ref[0, :, p * LANES:(p + 1) * LANES] for p in range(npair)]
    tri = tri_ref[...]
    row = lax.broadcasted_iota(jnp.int32, (tq, 2 * SB_KBLK), 0)
    col = lax.broadcasted_iota(jnp.int32, (tq, 2 * SB_KBLK), 1)
    ks = pl.multiple_of(i * SB_KBLK, SB_KBLK)
    zeros_c = [jnp.zeros((tq, 2 * SB_KBLK), F32)] * npair
    zeros_a = [jnp.zeros((tq, LANES), F32)] * npair
    c, a = _sb_step(qs, k_ref[0, pl.ds(ks, SB_KBLK), :], v_ref[0, pl.ds(ks, SB_KBLK), :], tri, zeros_c, zeros_a,
                    (col & (SB_KBLK - 1)) < row)
    for p in range(npair):
        carry_ref[p] = c[p]
        acc_ref[p] = a[p]

    def cond(s):
        j, live = s
        return jnp.logical_and(j >= 0, live > SB_DEAD_CARRY)

    def body(s):
        j, _ = s
        ks = pl.multiple_of(j * SB_KBLK, SB_KBLK)
        c, a = _sb_step(qs, k_ref[0, pl.ds(ks, SB_KBLK), :], v_ref[0, pl.ds(ks, SB_KBLK), :], tri,
                        [carry_ref[p] for p in range(npair)], [acc_ref[p] for p in range(npair)], None)
        for p in range(npair):
            carry_ref[p] = c[p]
            acc_ref[p] = a[p]
        return j - 1, jnp.max(functools.reduce(jnp.maximum, c))

    lax.while_loop(cond, body, (i - 1, jnp.max(functools.reduce(jnp.maximum, c))))
    for p in range(npair):
        o_ref[0, :, p * LANES:(p + 1) * LANES] = acc_ref[p].astype(o_ref.dtype)


def _sb_prompt(q, kb, vb, tri):
    B, T, W = q.shape
    tq = SB_KBLK
    npair = W // LANES
    qspec = pl.BlockSpec((1, tq, W), lambda b, i: (b, i, 0))
    kspec = pl.BlockSpec((1, T, W), lambda b, i: (b, 0, 0))
    return pl.pallas_call(
        functools.partial(_sb_prompt_kernel, tq=tq), out_shape=jax.ShapeDtypeStruct((B, T, W), BF16), grid=(B, T // tq),
        in_specs=[qspec, kspec, kspec, _const_spec(tri.shape)], out_specs=qspec,
        scratch_shapes=[pltpu.VMEM((npair, tq, 2 * SB_KBLK), F32), pltpu.VMEM((npair, tq, LANES), F32)],
        compiler_params=_params(("parallel", "arbitrary")), name="sb_prompt")(q, kb, vb, tri)


def _sb_sample_kernel(q_ref, k_ref, v_ref, tri_ref, o_ref, *, tq, nkb, qpos0):
    npair = q_ref.shape[2] // LANES
    qs = [q_ref[0, :, p * LANES:(p + 1) * LANES] for p in range(npair)]
    tri = tri_ref[...]
    row = lax.broadcasted_iota(jnp.int32, (tq, 2 * SB_KBLK), 0)
    col = lax.broadcasted_iota(jnp.int32, (tq, 2 * SB_KBLK), 1)
    c = [jnp.zeros((tq, 2 * SB_KBLK), F32)] * npair
    a = [jnp.zeros((tq, LANES), F32)] * npair
    for j in range(nkb - 1, -1, -1):
        ks = j * SB_KBLK
        c, a = _sb_step(qs, k_ref[0, ks:ks + SB_KBLK, :], v_ref[0, ks:ks + SB_KBLK, :], tri, c, a,
                        (ks + (col & (SB_KBLK - 1))) < (qpos0 + row))
    for p in range(npair):
        o_ref[0, :, p * LANES:(p + 1) * LANES] = a[p].astype(o_ref.dtype)


def _sb_sample(q, k_all, v_all, tri, *, qpos0):
    B, tq, W = q.shape
    Tk = k_all.shape[1]
    qspec = pl.BlockSpec((1, tq, W), lambda b: (b, 0, 0))
    kspec = pl.BlockSpec((1, Tk, W), lambda b: (b, 0, 0))
    return pl.pallas_call(
        functools.partial(_sb_sample_kernel, tq=tq, nkb=Tk // SB_KBLK, qpos0=qpos0),
        out_shape=jax.ShapeDtypeStruct((B, tq, W), BF16), grid=(B,),
        in_specs=[qspec, kspec, kspec, _const_spec(tri.shape)], out_specs=qspec,
        compiler_params=_params(("parallel",)), name="sb_sample")(q, k_all, v_all, tri)


def _mla_prompt_kernel(q_ref, kt_ref, vx_ref, o_ref, s_ref, p_ref, m_ref, a_ref, acc_ref, *, tq, tk, kvlora):
    i = pl.program_id(1)
    H = q_ref.shape[1]
    kc = q_ref.shape[3]
    group_heads = H // MLA_ROW_GROUPS
    gr = group_heads * tq
    m_ref[...] = jnp.full(m_ref.shape, -jnp.inf, F32)
    acc_ref[...] = jnp.zeros(acc_ref.shape, F32)

    def step(j, masked):
        ks = pl.multiple_of(j * tk, tk)
        kt = kt_ref[0, :, pl.ds(ks, tk)]
        vx = vx_ref[0, pl.ds(ks, tk), :]
        for g in range(MLA_ROW_GROUPS):
            qg = q_ref[0, g * group_heads:(g + 1) * group_heads].reshape(gr, kc)
            s_ref[g * gr:(g + 1) * gr, :] = _dot(qg, kt)
        def score_tiles(r0):
            rs = slice(r0, r0 + MLA_ROW_CHUNK)
            tiles = [s_ref[rs, t * LANES:(t + 1) * LANES] for t in range(tk // LANES)]
            if masked:
                row = lax.broadcasted_iota(jnp.int32, (MLA_ROW_CHUNK, LANES), 0)
                col = lax.broadcasted_iota(jnp.int32, (MLA_ROW_CHUNK, LANES), 1)
                qchunk = (i * tq + (r0 % tq) + row) >> CHUNK_SHIFT
                tiles = [jnp.where(((j * tk + t * LANES + col) >> CHUNK_SHIFT) <= qchunk, st, -jnp.inf)
                         for t, st in enumerate(tiles)]
            return tiles

        for g in range(MLA_ROW_GROUPS):
            for c in range(gr // MLA_ROW_CHUNK):
                r0 = g * gr + c * MLA_ROW_CHUNK
                rs = slice(r0, r0 + MLA_ROW_CHUNK)
                tiles = score_tiles(r0)
                m_old = m_ref[rs, :]
                m_new = jnp.maximum(m_old, jnp.max(functools.reduce(jnp.maximum, tiles), axis=-1, keepdims=True))
                for t, st in enumerate(tiles):
                    p_ref[rs, t * LANES:(t + 1) * LANES] = jnp.exp(st - m_new).astype(BF16)
                a_ref[rs, :] = jnp.exp(m_old - m_new)
                m_ref[rs, :] = m_new
            gs = slice(g * gr, (g + 1) * gr)
            alpha = a_ref[gs, :]
            acc_ref[gs, :] = jnp.concatenate([alpha, alpha], axis=-1) * acc_ref[gs, :] + _dot(p_ref[gs, :], vx)

    nfull = (i * tq) // tk

    def body(j, c):
        step(j, False)
        return c

    lax.fori_loop(0, nfull, body, 0)
    for d in range(tq // tk):
        step(nfull + d, True)
    for h in range(H):
        hs = slice(h * tq, (h + 1) * tq)
        o_ref[0, :, h * kvlora:(h + 1) * kvlora] = (acc_ref[hs, :kvlora] / acc_ref[hs, kvlora:]).astype(o_ref.dtype)


def _mla_prompt(qcat, kcat_t, vext, *, tq, tk, kvlora):
    B, H, T, KC = qcat.shape
    grid = (B, T // tq)
    rows = H * tq
    return pl.pallas_call(
        functools.partial(_mla_prompt_kernel, tq=tq, tk=tk, kvlora=kvlora),
        out_shape=jax.ShapeDtypeStruct((B, T, H * kvlora), BF16), grid=grid,
        in_specs=[pl.BlockSpec((1, H, tq, KC), lambda b, i: (b, 0, i, 0)), pl.BlockSpec((1, KC, T), lambda b, i: (b, 0, 0)),
                  pl.BlockSpec((1, T, 2 * kvlora), lambda b, i: (b, 0, 0))],
        out_specs=pl.BlockSpec((1, tq, H * kvlora), lambda b, i: (b, i, 0)),
        scratch_shapes=[pltpu.VMEM((rows, tk), F32), pltpu.VMEM((rows, tk), BF16), pltpu.VMEM((rows, LANES), F32),
                        pltpu.VMEM((rows, LANES), F32), pltpu.VMEM((rows, 2 * kvlora), F32)],
        compiler_params=_params(("parallel", "arbitrary")), name="mla_prompt")(qcat, kcat_t, vext)


def _mla_sample_kernel(q_ref, kc_ref, o_ref, *, tq, qpos0, nvalid, kvlora):
    H = q_ref.shape[1]
    q = q_ref[0].reshape(H * tq, q_ref.shape[3])
    kall = kc_ref[0]
    s = _dot_nt(q, kall)
    row = lax.broadcasted_iota(jnp.int32, s.shape, 0)
    col = lax.broadcasted_iota(jnp.int32, s.shape, 1)
    qpos = qpos0 + (row & (tq - 1))
    keep = jnp.logical_and((col >> CHUNK_SHIFT) <= (qpos >> CHUNK_SHIFT), col < nvalid)
    s = jnp.where(keep, s, -jnp.inf)
    p = jnp.exp(s - jnp.max(s, axis=-1, keepdims=True))
    out = _dot(p.astype(BF16), kall[:, :kvlora]) / jnp.sum(p, axis=-1, keepdims=True)
    for h in range(H):
        o_ref[0, :, h * kvlora:(h + 1) * kvlora] = out[h * tq:(h + 1) * tq].astype(o_ref.dtype)


def _mla_sample(qcat, kcat_all, *, qpos0, nvalid, kvlora):
    B, H, tq, KC = qcat.shape
    Tk = kcat_all.shape[1]
    return pl.pallas_call(
        functools.partial(_mla_sample_kernel, tq=tq, qpos0=qpos0, nvalid=nvalid, kvlora=kvlora),
        out_shape=jax.ShapeDtypeStruct((B, tq, H * kvlora), BF16), grid=(B,),
        in_specs=[pl.BlockSpec((1, H, tq, KC), lambda b: (b, 0, 0, 0)), pl.BlockSpec((1, Tk, KC), lambda b: (b, 0, 0))],
        out_specs=pl.BlockSpec((1, tq, H * kvlora), lambda b: (b, 0, 0)),
        compiler_params=_params(("parallel",)), name="mla_sample")(qcat, kcat_all)


def _router_logits(xf_bf, wrh_ref, wrl_ref, br_ref):
    return _dot(xf_bf, wrh_ref[...]) + _dot(xf_bf, wrl_ref[...]) + br_ref[...]


def _route(logits, gidx=None):
    lane = lax.broadcasted_iota(jnp.int32, logits.shape, 1)
    ninf = -jnp.inf
    gl = jnp.where(lane < N_GROUPS, logits, ninf)
    gmax = jnp.max(gl, axis=-1, keepdims=True)
    gexp = jnp.where(lane < N_GROUPS, jnp.exp(logits - gmax), 0.0)
    if gidx is None:
        gidx = jnp.min(jnp.where(gl == gmax, lane, LANES), axis=-1, keepdims=True)
    g_val = jnp.sum(jnp.where(lane == gidx, gexp, 0.0), axis=-1, keepdims=True) / jnp.sum(gexp, axis=-1, keepdims=True)
    lo = ROUTER_LANE0 + EXPERTS_PER_GROUP * gidx
    el = jnp.where(jnp.logical_and(lane >= lo, lane < lo + EXPERTS_PER_GROUP), logits, ninf)
    v1 = jnp.max(el, axis=-1, keepdims=True)
    i1 = jnp.min(jnp.where(el == v1, lane, LANES), axis=-1, keepdims=True)
    el2 = jnp.where(lane == i1, ninf, el)
    v2 = jnp.max(el2, axis=-1, keepdims=True)
    i2 = jnp.min(jnp.where(el2 == v2, lane, LANES), axis=-1, keepdims=True)
    e2 = jnp.exp(v2 - v1)
    den = 1.0 + e2
    return jnp.where(lane == i1, (1.0 / den) * g_val, 0.0) + jnp.where(lane == i2, (e2 / den) * g_val, 0.0), gidx


def _merge_kernel(x_ref, sb_ref, lat_ref, ln1_ref, wg_ref, wuv_ref, wsb_ref, wmla_ref, wout_ref, ln2_ref, wrh_ref, wrl_ref,
                  br_ref, tril_ref, h_ref, xf_ref, comb_ref, info_ref, cnt_ref, *, d):
    @pl.when(pl.program_id(0) == 0)
    def _():
        cnt_ref[...] = jnp.zeros(cnt_ref.shape, F32)

    x = x_ref[...]
    xn = _rms(x, ln1_ref[...]).astype(BF16)
    g = _dot(xn, wg_ref[...])
    mla_o = _dot(lat_ref[...], wuv_ref[...]).astype(BF16)
    merged = jax.nn.sigmoid(g[:, :d]) * _dot(sb_ref[...], wsb_ref[...]) + jax.nn.sigmoid(g[:, d:]) * _dot(mla_o, wmla_ref[...])
    h = x + _dot(merged.astype(BF16), wout_ref[...])
    h_ref[...] = h
    xf = _rms(h, ln2_ref[...]).astype(BF16)
    xf_ref[...] = xf
    comb, gidx = _route(_router_logits(xf, wrh_ref, wrl_ref, br_ref))
    comb_ref[...] = comb
    lane = lax.broadcasted_iota(jnp.int32, comb.shape, 1)
    onehot = lane == gidx
    before = _dot(tril_ref[...], jnp.where(onehot, 1.0, 0.0).astype(BF16)) + cnt_ref[...]
    rank = jnp.sum(jnp.where(onehot, before, 0.0), axis=-1, keepdims=True).astype(jnp.int32)
    info_ref[...] = jnp.where(lane == 0, rank, jnp.where(lane == 1, gidx, 0))
    cnt_ref[...] += jnp.sum(jnp.where(onehot, 1.0, 0.0), axis=0, keepdims=True)


def _merge(x, sb_o, lat, ln1, wg, wuv, wsb, wmla, wout, ln2, wrh, wrl, br, tril, *, tm):
    N, D = x.shape
    tok = lambda w: pl.BlockSpec((tm, w), lambda i: (i, 0))
    consts = (ln1, wg, wuv, wsb, wmla, wout, ln2, wrh, wrl, br, tril)
    return pl.pallas_call(
        functools.partial(_merge_kernel, d=D),
        out_shape=(jax.ShapeDtypeStruct((N, D), F32), jax.ShapeDtypeStruct((N, D), BF16), jax.ShapeDtypeStruct((N, LANES), F32),
                   jax.ShapeDtypeStruct((N, LANES), jnp.int32), jax.ShapeDtypeStruct((1, LANES), F32)),
        grid=(N // tm,), in_specs=[tok(D), tok(sb_o.shape[1]), tok(lat.shape[1])] + [_const_spec(a.shape) for a in consts],
        out_specs=(tok(D), tok(D), tok(LANES), tok(LANES), _const_spec((1, LANES))),
        compiler_params=_params(("arbitrary",)), name="merge")(x, sb_o, lat, *consts)


def _moe_kernel(xf_ref, comb_ref, h_ref, wg_ref, wu_ref, wd_ref, lnf_ref, y_ref, acc_ref, *, final_norm):
    e = pl.program_id(1)

    @pl.when(e == 0)
    def _():
        acc_ref[...] = jnp.zeros(acc_ref.shape, F32)

    xf = xf_ref[...]
    comb = comb_ref[...]
    lane = lax.broadcasted_iota(jnp.int32, comb.shape, 1)
    c = jnp.sum(jnp.where(lane == e + ROUTER_LANE0, comb, 0.0), axis=-1, keepdims=True)
    act = jax.nn.silu(_dot(xf, wg_ref[0])) * _dot(xf, wu_ref[0])
    acc_ref[...] += _dot((act * c).astype(BF16), wd_ref[0])

    @pl.when(e == pl.num_programs(1) - 1)
    def _():
        y = h_ref[...] + acc_ref[...]
        y_ref[...] = _rms(y, lnf_ref[...]) if final_norm else y


def _moe(xf, comb, h, wg, wu, wd, lnf, *, tm, final_norm):
    N, D = h.shape
    E, _, F = wg.shape
    tok = lambda w: pl.BlockSpec((tm, w), lambda i, e: (i, 0))
    return pl.pallas_call(
        functools.partial(_moe_kernel, final_norm=final_norm), out_shape=jax.ShapeDtypeStruct((N, D), F32), grid=(N // tm, E),
        in_specs=[tok(D), tok(LANES), tok(D), pl.BlockSpec((1, D, F), lambda i, e: (e, 0, 0)),
                  pl.BlockSpec((1, D, F), lambda i, e: (e, 0, 0)), pl.BlockSpec((1, F, D), lambda i, e: (e, 0, 0)),
                  pl.BlockSpec((1, D), lambda i, e: (0, 0))],
        out_specs=tok(D), scratch_shapes=[pltpu.VMEM((tm, D), F32)],
        compiler_params=_params(("parallel", "arbitrary")), name="moe")(xf, comb, h, wg, wu, wd, lnf)


def _to_row_tiles(ref, x):
    for k in range(ref.shape[1]):
        ref[:, k, :] = x[:, k * LANES:(k + 1) * LANES]


def _from_row_tiles(ref):
    return jnp.concatenate([ref[:, k, :] for k in range(ref.shape[1])], axis=-1)


def _row_dma_loop(n, make_copy):
    unroll = 8

    def issue(k, c):
        for u in range(unroll):
            make_copy(k * unroll + u).start(priority=u % 2)
        return c

    def drain(r, c):
        make_copy(0).wait()
        return c

    lax.fori_loop(0, n // unroll, issue, 0)
    lax.fori_loop(0, n, drain, 0, unroll=unroll)


def _dispatch_kernel(pos_hbm, xf_ref, init_hbm, out_hbm, idx_ref, buf_ref, sem_idx, sem_rows, *, tm):
    del init_hbm
    idx_copy = pltpu.make_async_copy(pos_hbm.at[pl.program_id(0)], idx_ref, sem_idx)
    idx_copy.start()
    _to_row_tiles(buf_ref, xf_ref[...].astype(F32))
    idx_copy.wait()
    _row_dma_loop(tm, lambda r: pltpu.make_async_copy(buf_ref.at[r], out_hbm.at[idx_ref[0, r]], sem_rows))


def _dispatch(pos, xf, n_rows, *, tm):
    N, D = xf.shape
    tiles = D // LANES
    init = jnp.zeros((n_rows, tiles, LANES), F32)
    anyspec = pl.BlockSpec(memory_space=pl.ANY)
    return pl.pallas_call(
        functools.partial(_dispatch_kernel, tm=tm), out_shape=jax.ShapeDtypeStruct(init.shape, F32), grid=(N // tm,),
        in_specs=[anyspec, pl.BlockSpec((tm, D), lambda i: (i, 0)), anyspec], out_specs=anyspec,
        scratch_shapes=[pltpu.SMEM((1, tm), jnp.int32), pltpu.VMEM((tm, tiles, LANES), F32), pltpu.SemaphoreType.DMA,
                        pltpu.SemaphoreType.DMA],
        input_output_aliases={2: 0}, compiler_params=_params(("arbitrary",)), name="moe_dispatch")(
        pos.reshape(N // tm, 1, tm), xf, init)


def _moe_grouped_kernel(grp_ref, valid_ref, xs_ref, wg_ref, wu_ref, wd_ref, wrh_ref, wrl_ref, br_ref, ys_ref):
    w = pl.program_id(0)
    grp = grp_ref[w]

    @pl.when(valid_ref[w] == 0)
    def _():
        ys_ref[...] = jnp.zeros(ys_ref.shape, F32)

    @pl.when(valid_ref[w] == 1)
    def _():
        x = _from_row_tiles(xs_ref).astype(BF16)
        comb = _route(_router_logits(x, wrh_ref, wrl_ref, br_ref), grp)[0]
        lane = lax.broadcasted_iota(jnp.int32, comb.shape, 1)
        lane0 = ROUTER_LANE0 + EXPERTS_PER_GROUP * grp
        out = None
        for e in range(EXPERTS_PER_GROUP):
            c = jnp.sum(jnp.where(lane == lane0 + e, comb, 0.0), axis=-1, keepdims=True)
            act = jax.nn.silu(_dot(x, wg_ref[e])) * _dot(x, wu_ref[e])
            y = _dot((act * c).astype(BF16), wd_ref[e])
            out = y if out is None else out + y
        _to_row_tiles(ys_ref, out)


def _moe_grouped(grp, valid, xs, wg, wu, wd, wrh, wrl, br, *, tm):
    S, tiles, _ = xs.shape
    D = tiles * LANES
    F = wg.shape[2]
    rows = pl.BlockSpec((tm, tiles, LANES), lambda w, grp, valid: (w, 0, 0))
    group = lambda w, grp, valid: (grp[w], 0, 0)
    const = lambda shape: pl.BlockSpec(shape, lambda w, grp, valid: (0,) * len(shape))
    grid_spec = pltpu.PrefetchScalarGridSpec(
        num_scalar_prefetch=2, grid=(S // tm,),
        in_specs=[rows, pl.BlockSpec((EXPERTS_PER_GROUP, D, F), group), pl.BlockSpec((EXPERTS_PER_GROUP, D, F), group),
                  pl.BlockSpec((EXPERTS_PER_GROUP, F, D), group), const(wrh.shape), const(wrl.shape), const(br.shape)],
        out_specs=rows)
    return pl.pallas_call(_moe_grouped_kernel, out_shape=jax.ShapeDtypeStruct(xs.shape, F32), grid_spec=grid_spec,
                          compiler_params=_params(("arbitrary",)), name="moe_grouped")(
        grp, valid, xs, wg, wu, wd, wrh, wrl, br)


def _combine_kernel(pos_hbm, h_ref, lnf_ref, ys_hbm, y_ref, idx_ref, buf_ref, sem_idx, sem_rows, *, tm, final_norm):
    idx_copy = pltpu.make_async_copy(pos_hbm.at[pl.program_id(0)], idx_ref, sem_idx)
    idx_copy.start()
    idx_copy.wait()
    _row_dma_loop(tm, lambda r: pltpu.make_async_copy(ys_hbm.at[idx_ref[0, r]], buf_ref.at[r], sem_rows))
    y = h_ref[...] + _from_row_tiles(buf_ref)
    y_ref[...] = _rms(y, lnf_ref[...]) if final_norm else y


def _combine(pos, h, lnf, ys, *, tm, final_norm):
    N, D = h.shape
    tiles = D // LANES
    anyspec = pl.BlockSpec(memory_space=pl.ANY)
    return pl.pallas_call(
        functools.partial(_combine_kernel, tm=tm, final_norm=final_norm), out_shape=jax.ShapeDtypeStruct((N, D), F32),
        grid=(N // tm,), in_specs=[anyspec, pl.BlockSpec((tm, D), lambda i: (i, 0)), _const_spec(lnf.shape), anyspec],
        out_specs=pl.BlockSpec((tm, D), lambda i: (i, 0)),
        scratch_shapes=[pltpu.SMEM((1, tm), jnp.int32), pltpu.VMEM((tm, tiles, LANES), F32), pltpu.SemaphoreType.DMA,
                        pltpu.SemaphoreType.DMA],
        compiler_params=_params(("arbitrary",)), name="moe_combine")(pos.reshape(N // tm, 1, tm), h, lnf, ys)


def _group_layout(info, cnt, *, tm, n_tiles):
    rank, gidx = info[:, 0], info[:, 1]
    counts = cnt[0, :N_GROUPS].astype(jnp.int32)
    ntiles = (counts + tm - 1) // tm
    tile_end = jnp.cumsum(ntiles)
    pos = (tile_end - ntiles)[gidx] * tm + rank
    tile = jnp.arange(n_tiles, dtype=jnp.int32)
    valid = (tile < tile_end[-1]).astype(jnp.int32)
    grp = jnp.sum((jnp.minimum(tile, tile_end[-1] - 1)[:, None] >= tile_end[None, :]).astype(jnp.int32), axis=1)
    return pos, grp, valid


def _pad_lanes(w):
    return jnp.pad(w, [(0, 0)] * (w.ndim - 1) + [(0, LANES - w.shape[-1])])


def _swap_halves(w):
    half = w.shape[-1] // 2
    return jnp.concatenate([w[..., half:], w[..., :half]], axis=-1)


def _rope_tables(pos, rope):
    half = rope // 2
    inv_freq = ROPE_BASE ** (-jnp.arange(half, dtype=F32) / half)
    ang = pos.astype(F32)[:, None] * inv_freq[None, :]
    cos, sin = jnp.cos(ang), jnp.sin(ang)
    return _pad_lanes(jnp.concatenate([cos, cos], axis=-1)), _pad_lanes(jnp.concatenate([-sin, sin], axis=-1))


def _sb_scan_operator():
    n = SB_KBLK
    tri = (jnp.arange(n)[:, None] >= jnp.arange(n)[None, :]).astype(F32)
    z = jnp.zeros((n, n), F32)
    return jnp.concatenate([jnp.concatenate([tri, z], axis=1), jnp.concatenate([z, tri], axis=1)], axis=0).astype(BF16)


def _strict_lower(n):
    return (jnp.arange(n)[:, None] > jnp.arange(n)[None, :]).astype(BF16)


def _pad_rows(a, rows):
    return jnp.pad(a, [(0, 0), (0, rows - a.shape[1]), (0, 0)])


def kernel(x_prompt, x_sample, cache_sb_k, cache_sb_v, cache_mla_ckv, cache_mla_kpe, ln1, w_in, q_norm, w_q_up, kv_norm, w_uk,
           w_uv, w_sb_branch, w_mla_branch, w_out, ln2, w_group, b_group, w_router, b_router, w_gate, w_up, w_down, ln_f):
    B, T, D = x_prompt.shape
    SBt, SQ, _ = x_sample.shape
    depth = ln1.shape[0]
    past = cache_sb_k.shape[2]
    sbw = cache_sb_k.shape[3] * cache_sb_k.shape[4]
    sb_hd = cache_sb_k.shape[4]
    qlora = q_norm.shape[1]
    kvlora = kv_norm.shape[1]
    rope = cache_mla_kpe.shape[3]
    nope = w_uk.shape[3]
    vdim = w_uv.shape[3]
    H = w_uk.shape[2]
    sb_scale = 1.0 / math.sqrt(sb_hd)
    mla_scale = 1.0 / math.sqrt(nope + rope)

    cos_p, sin_p = _rope_tables(jnp.arange(T, dtype=jnp.int32), rope)
    pos_s = past + jnp.arange(SQ, dtype=jnp.int32)
    cos_s, sin_s = _rope_tables(jnp.tile(pos_s, SBt), rope)
    tri = _sb_scan_operator()
    eye = jnp.eye(H, dtype=F32)
    row = lambda a: a.reshape(1, -1)

    tm_p = min(512, T)
    tq_mla = min(512, T)
    ns = SBt * SQ
    tk_all = -(-(past + SQ) // SB_KBLK) * SB_KBLK

    xp = x_prompt
    xs = x_sample.reshape(1, ns, D)
    outs_p = [[], [], [], []]
    outs_s = [[], [], [], []]
    for l in range(depth):
        wl = w_in[l]
        c = 3 * sbw + qlora + kvlora
        wkpe = wl[:, c:c + rope]
        wmain = jnp.concatenate([wl[:, :c], _pad_lanes(wkpe), _pad_lanes(_swap_halves(wkpe))], axis=1).astype(BF16)
        wgates = wl[:, c + rope:].astype(BF16)
        wq = w_q_up[l]
        wqn = wq[:, :, :nope].reshape(qlora, H * nope).astype(BF16)
        wqpe = _pad_lanes(wq[:, :, nope:]).reshape(qlora, H * LANES).astype(BF16)
        wqpes = _pad_lanes(_swap_halves(wq[:, :, nope:])).reshape(qlora, H * LANES).astype(BF16)
        wuk_bd = jnp.einsum('hdc,hg->hdgc', jnp.transpose(w_uk[l], (1, 2, 0)), eye).reshape(H * nope, H * kvlora).astype(BF16)
        wuv_bd = jnp.einsum('hcd,hg->hcgd', jnp.transpose(w_uv[l], (1, 0, 2)), eye).reshape(H * kvlora, H * vdim).astype(BF16)
        wr = _pad_lanes(jnp.concatenate([w_group[l], w_router[l]], axis=1))
        wrh = wr.astype(BF16)
        wrl = (wr - wrh.astype(F32)).astype(BF16)
        br = _pad_lanes(jnp.concatenate([b_group[l], b_router[l]]).reshape(1, -1))
        wsb = w_sb_branch[l].astype(BF16)
        wmla = w_mla_branch[l].astype(BF16)
        wo = w_out[l].astype(BF16)
        wg_e = w_gate[l].astype(BF16)
        wu_e = w_up[l].astype(BF16)
        wd_e = w_down[l].astype(BF16)
        lnf = row(ln_f)
        last = l == depth - 1
        proj_args = (row(ln1[l]), wmain, row(q_norm[l]), row(kv_norm[l]), wqn, wuk_bd, wqpe, wqpes)
        proj_kw = dict(sb_scale=sb_scale, mla_scale=mla_scale, sbw=sbw, qlora=qlora, kvlora=kvlora, rope=rope)
        merge_args = (row(ln1[l]), wgates, wuv_bd, wsb, wmla, wo, row(ln2[l]), wrh, wrl, br)

        def moe_block(h, xf, comb, info, cnt):
            n = h.shape[0]
            if n < MOE_GROUPED_MIN_TOKENS:
                return _moe(xf, comb, h, wg_e, wu_e, wd_e, lnf, tm=n, final_norm=last)
            tm_g = min(MOE_GROUP_TILE, n // N_GROUPS)
            tm_tok = min(MOE_ROW_DMA_TILE, n)
            n_tiles = n // tm_g + N_GROUPS
            pos, grp, valid = _group_layout(info, cnt, tm=tm_g, n_tiles=n_tiles)
            xs_rows = _dispatch(pos, xf, n_tiles * tm_g, tm=tm_tok)
            ys_rows = _moe_grouped(grp, valid, xs_rows, wg_e, wu_e, wd_e, wrh, wrl, br, tm=tm_g)
            return _combine(pos, h, lnf, ys_rows, tm=tm_tok, final_norm=last)

        q, k, v, kb, vb, ckv, kpe, kcat, vext, qcat = _proj(xp, cos_p, sin_p, *proj_args, tm=tm_p, **proj_kw)
        sb_o = _sb_prompt(q, kb, vb, tri)
        lat = _mla_prompt(qcat, jnp.swapaxes(kcat, 1, 2), vext, tq=tq_mla, tk=tq_mla, kvlora=kvlora)
        y = moe_block(*_merge(xp.reshape(B * T, D), sb_o.reshape(B * T, sbw), lat.reshape(B * T, H * kvlora), *merge_args,
                              _strict_lower(tm_p), tm=tm_p))
        xp = y.reshape(B, T, D)
        for lst, a in zip(outs_p, (k, v, ckv, kpe)):
            lst.append(a)

        q2, k2, v2, kb2, vb2, ckv2, kpe2, kcat2, _, qcat2 = _proj(xs, cos_s, sin_s, *proj_args, tm=ns, **proj_kw)
        per_stream = lambda a: a.reshape(SBt, SQ, a.shape[-1])
        k_all = _pad_rows(jnp.concatenate([cache_sb_k[l].reshape(SBt, past, sbw).astype(BF16), per_stream(kb2)], axis=1), tk_all)
        v_all = _pad_rows(jnp.concatenate([cache_sb_v[l].reshape(SBt, past, sbw).astype(BF16), per_stream(vb2)], axis=1), tk_all)
        kc_past = jnp.concatenate([cache_mla_ckv[l], _pad_lanes(cache_mla_kpe[l])], axis=-1).astype(BF16)
        kcat_all = _pad_rows(jnp.concatenate([kc_past, per_stream(kcat2)], axis=1), tk_all)
        sb_o2 = _sb_sample(per_stream(q2), k_all, v_all, tri, qpos0=past)
        qcat_s = jnp.transpose(qcat2.reshape(H, SBt, SQ, kvlora + LANES), (1, 0, 2, 3))
        lat2 = _mla_sample(qcat_s, kcat_all, qpos0=past, nvalid=past + SQ, kvlora=kvlora)
        y2 = moe_block(*_merge(xs.reshape(ns, D), sb_o2.reshape(ns, sbw), lat2.reshape(ns, H * kvlora), *merge_args,
                              _strict_lower(ns), tm=ns))
        xs = y2.reshape(1, ns, D)
        for lst, a in zip(outs_s, (k2, v2, ckv2, kpe2)):
            lst.append(a)

    heads = lambda a, n: a.reshape(n, -1, SB_HEADS, sb_hd)
    return (xp, xs.reshape(SBt, SQ, D),
            jnp.stack([heads(a, B) for a in outs_p[0]]), jnp.stack([heads(a, B) for a in outs_p[1]]),
            jnp.stack(outs_p[2]), jnp.stack(outs_p[3]),
            jnp.stack([heads(a.reshape(SBt, SQ, sbw), SBt) for a in outs_s[0]]),
            jnp.stack([heads(a.reshape(SBt, SQ, sbw), SBt) for a in outs_s[1]]),
            jnp.stack([a.reshape(SBt, SQ, kvlora) for a in outs_s[2]]),
            jnp.stack([a.reshape(SBt, SQ, rope) for a in outs_s[3]]))
```

```python
import functools
import math

import jax
import jax.numpy as jnp
from jax import lax
from jax.experimental import pallas as pl
from jax.experimental.pallas import tpu as pltpu

F32 = jnp.float32
BF16 = jnp.bfloat16

CHUNK = 64
CHUNK_SHIFT = CHUNK.bit_length() - 1
assert 1 << CHUNK_SHIFT == CHUNK
SB_HEADS = 8
MLA_HEADS = 8
ROPE_BASE = 10000.0
N_GROUPS = 4
EXPERTS_PER_GROUP = 8
N_EXPERTS = N_GROUPS * EXPERTS_PER_GROUP
RMS_EPS = 1e-6

LANES = 128
SB_KBLK = LANES
SB_DEAD_CARRY = -110.0
MLA_ROW_GROUPS = 8
MLA_ROW_CHUNK = 64
VMEM_LIMIT = 56 * 1024 * 1024
ROUTER_LANE0 = N_GROUPS
MOE_GROUP_TILE = 512
MOE_ROW_DMA_TILE = 2048
MOE_GROUPED_MIN_TOKENS = 2048


def _dot(a, b):
    return jnp.dot(a, b, preferred_element_type=F32)


def _dot_nt(a, b):
    return lax.dot_general(a, b, (((1,), (1,)), ((), ())), preferred_element_type=F32)


def _rms(x, g):
    return x * lax.rsqrt(jnp.mean(x * x, axis=-1, keepdims=True) + RMS_EPS) * g


def _params(sem):
    return pltpu.CompilerParams(dimension_semantics=sem, vmem_limit_bytes=VMEM_LIMIT)


def _const_spec(shape):
    return pl.BlockSpec(shape, lambda *_: (0,) * len(shape))


def _proj_kernel(x_ref, cos_ref, sin_ref, ln1_ref, wmain_ref, qn_ref, kvn_ref, wqn_ref, wuk_ref, wqpe_ref, wqpes_ref,
                 q_ref, k_ref, v_ref, kb_ref, vb_ref, ckv_ref, kpe_ref, kcat_ref, kcatt_ref, vext_ref, qcat_ref, *, sb_scale, mla_scale, sbw,
                 qlora, kvlora, rope):
    xn = _rms(x_ref[0], ln1_ref[...]).astype(BF16)
    p = _dot(xn, wmain_ref[...])
    o = 0
    q = p[:, o:o + sbw]; o += sbw
    k = p[:, o:o + sbw]; o += sbw
    v = p[:, o:o + sbw]; o += sbw
    cq = p[:, o:o + qlora]; o += qlora
    ckv = p[:, o:o + kvlora]; o += kvlora
    kpe = p[:, o:o + LANES]; o += LANES
    kpe_sw = p[:, o:o + LANES]
    q_ref[0] = (q * sb_scale).astype(BF16)
    k_ref[0] = k
    v_ref[0] = v
    kb_ref[0] = k.astype(BF16)
    vb_ref[0] = v.astype(BF16)
    cos = cos_ref[...]
    sin = sin_ref[...]
    ckvn = _rms(ckv, kvn_ref[...])
    ckv_ref[0] = ckvn
    kpe_rot = kpe * cos + kpe_sw * sin
    kpe_ref[0] = kpe_rot[:, :rope]
    kcat = jnp.concatenate([ckvn, kpe_rot], axis=-1)
    kcat_ref[0] = kcat.astype(BF16)
    kcatt_ref[0] = kcat.T.astype(BF16)
    vext_ref[0] = jnp.concatenate([ckvn, jnp.ones_like(ckvn)], axis=-1).astype(BF16)
    cqn = _rms(cq, qn_ref[...]).astype(BF16)
    qnope = _dot(cqn, wqn_ref[...]).astype(BF16)
    qlat = _dot(qnope, wuk_ref[...])
    qpe = _dot(cqn, wqpe_ref[...])
    qpe_sw = _dot(cqn, wqpes_ref[...])
    for h in range(MLA_HEADS):
        sl = slice(h * LANES, (h + 1) * LANES)
        qcat_ref[0, h, :, :kvlora] = (qlat[:, h * kvlora:(h + 1) * kvlora] * mla_scale).astype(BF16)
        qcat_ref[0, h, :, kvlora:] = ((qpe[:, sl] * cos + qpe_sw[:, sl] * sin) * mla_scale).astype(BF16)


def _proj(x, cos_t, sin_t, ln1, wmain, qn, kvn, wqn, wuk, wqpe, wqpes, *, tm, sb_scale, mla_scale, sbw, qlora, kvlora, rope):
    B, T, D = x.shape
    grid = (B, T // tm)
    tok = lambda w: pl.BlockSpec((1, tm, w), lambda b, i: (b, i, 0))
    tab = pl.BlockSpec((tm, LANES), lambda b, i: (i, 0))
    kcw = kvlora + LANES
    out_shape = (
        jax.ShapeDtypeStruct((B, T, sbw), BF16),
        jax.ShapeDtypeStruct((B, T, sbw), F32),
        jax.ShapeDtypeStruct((B, T, sbw), F32),
        jax.ShapeDtypeStruct((B, T, sbw), BF16),
        jax.ShapeDtypeStruct((B, T, sbw), BF16),
        jax.ShapeDtypeStruct((B, T, kvlora), F32),
        jax.ShapeDtypeStruct((B, T, rope), F32),
        jax.ShapeDtypeStruct((B, T, kcw), BF16),
        jax.ShapeDtypeStruct((B, kcw, T), BF16),
        jax.ShapeDtypeStruct((B, T, 2 * kvlora), BF16),
        jax.ShapeDtypeStruct((B, MLA_HEADS, T, kcw), BF16),
    )
    out_specs = (tok(sbw), tok(sbw), tok(sbw), tok(sbw), tok(sbw), tok(kvlora), tok(rope), tok(kcw),
                 pl.BlockSpec((1, kcw, tm), lambda b, i: (b, 0, i)), tok(2 * kvlora),
                 pl.BlockSpec((1, MLA_HEADS, tm, kcw), lambda b, i: (b, 0, i, 0)))
    in_specs = [tok(D), tab, tab] + [_const_spec(a.shape) for a in (ln1, wmain, qn, kvn, wqn, wuk, wqpe, wqpes)]
    kern = functools.partial(_proj_kernel, sb_scale=sb_scale, mla_scale=mla_scale, sbw=sbw, qlora=qlora, kvlora=kvlora,
                             rope=rope)
    return pl.pallas_call(kern, out_shape=out_shape, grid=grid, in_specs=in_specs, out_specs=out_specs,
                          compiler_params=_params(("parallel", "parallel")), name="proj")(
        x, cos_t, sin_t, ln1, wmain, qn, kvn, wqn, wuk, wqpe, wqpes)


def _head_pair_blockdiag(blk):
    lane = lax.broadcasted_iota(jnp.int32, blk.shape, 1)
    zero = jnp.zeros_like(blk)
    half = LANES // 2
    return jnp.concatenate([jnp.where(lane < half, blk, zero), jnp.where(lane >= half, blk, zero)], axis=0)


def _sb_step(qs, kblk, vblk, tri, carries, accs, keep):
    tq = qs[0].shape[0]
    lbs, loms, parts = [], [], []
    for p, q in enumerate(qs):
        z = _dot_nt(q, _head_pair_blockdiag(kblk[:, p * LANES:(p + 1) * LANES]))
        lb = jnp.minimum(z, 0.0) - jnp.log(1.0 + jnp.exp(-jnp.abs(z)))
        lom = lb - z
        if keep is not None:
            lom = jnp.where(keep, lom, 0.0)
        hi = lom.astype(BF16)
        lbs.append(lb)
        loms.append(lom)
        parts += [hi, (lom - hi.astype(F32)).astype(BF16)]
    r = _dot(jnp.concatenate(parts, axis=0), tri)
    new_c, new_a = [], []
    for p in range(len(qs)):
        incl = r[2 * p * tq:(2 * p + 1) * tq] + r[(2 * p + 1) * tq:(2 * p + 2) * tq]
        w = jnp.exp(lbs[p] + (incl - loms[p]) + carries[p])
        if keep is not None:
            w = jnp.where(keep, w, 0.0)
        new_a.append(accs[p] + _dot(w.astype(BF16), _head_pair_blockdiag(vblk[:, p * LANES:(p + 1) * LANES])))
        tot = jnp.concatenate([jnp.broadcast_to(incl[:, 0:1], (tq, SB_KBLK)),
                               jnp.broadcast_to(incl[:, SB_KBLK:SB_KBLK + 1], (tq, SB_KBLK))], axis=1)
        new_c.append(carries[p] + tot)
    return new_c, new_a


def _sb_prompt_kernel(q_ref, k_ref, v_ref, tri_ref, o_ref, carry_ref, acc_ref, *, tq):
    i = pl.program_id(1)
    npair = q_ref.shape[2] // LANES
    qs = [q_ref[0, :, p * LANES:(p + 1) * LANES] for p in range(npair)]
    tri = tri_ref[...]
    row = lax.broadcasted_iota(jnp.int32, (tq, 2 * SB_KBLK), 0)
    col = lax.broadcasted_iota(jnp.int32, (tq, 2 * SB_KBLK), 1)
    ks = pl.multiple_of(i * SB_KBLK, SB_KBLK)
    zeros_c = [jnp.zeros((tq, 2 * SB_KBLK), F32)] * npair
    zeros_a = [jnp.zeros((tq, LANES), F32)] * npair
    c, a = _sb_step(qs, k_ref[0, pl.ds(ks, SB_KBLK), :], v_ref[0, pl.ds(ks, SB_KBLK), :], tri, zeros_c, zeros_a,
                    (col & (SB_KBLK - 1)) < row)
    for p in range(npair):
        carry_ref[p] = c[p]
        acc_ref[p] = a[p]

    def cond(s):
        j, live = s
        return jnp.logical_and(j >= 0, live > SB_DEAD_CARRY)

    def body(s):
        j, _ = s
        ks = pl.multiple_of(j * SB_KBLK, SB_KBLK)
        c, a = _sb_step(qs, k_ref[0, pl.ds(ks, SB_KBLK), :], v_ref[0, pl.ds(ks, SB_KBLK), :], tri,
                        [carry_ref[p] for p in range(npair)], [acc_ref[p] for p in range(npair)], None)
        for p in range(npair):
            carry_ref[p] = c[p]
            acc_ref[p] = a[p]
        return j - 1, jnp.max(functools.reduce(jnp.maximum, c))

    lax.while_loop(cond, body, (i - 1, jnp.max(functools.reduce(jnp.maximum, c))))
    for p in range(npair):
        o_ref[0, :, p * LANES:(p + 1) * LANES] = acc_ref[p].astype(o_ref.dtype)


def _sb_prompt(q, kb, vb, tri):
    B, T, W = q.shape
    tq = SB_KBLK
    npair = W // LANES
    qspec = pl.BlockSpec((1, tq, W), lambda b, i: (b, i, 0))
    kspec = pl.BlockSpec((1, T, W), lambda b, i: (b, 0, 0))
    return pl.pallas_call(
        functools.partial(_sb_prompt_kernel, tq=tq), out_shape=jax.ShapeDtypeStruct((B, T, W), BF16), grid=(B, T // tq),
        in_specs=[qspec, kspec, kspec, _const_spec(tri.shape)], out_specs=qspec,
        scratch_shapes=[pltpu.VMEM((npair, tq, 2 * SB_KBLK), F32), pltpu.VMEM((npair, tq, LANES), F32)],
        compiler_params=_params(("parallel", "arbitrary")), name="sb_prompt")(q, kb, vb, tri)


def _sb_sample_kernel(q_ref, k_ref, v_ref, tri_ref, o_ref, *, tq, nkb, qpos0):
    npair = q_ref.shape[2] // LANES
    qs = [q_ref[0, :, p * LANES:(p + 1) * LANES] for p in range(npair)]
    tri = tri_ref[...]
    row = lax.broadcasted_iota(jnp.int32, (tq, 2 * SB_KBLK), 0)
    col = lax.broadcasted_iota(jnp.int32, (tq, 2 * SB_KBLK), 1)
    c = [jnp.zeros((tq, 2 * SB_KBLK), F32)] * npair
    a = [jnp.zeros((tq, LANES), F32)] * npair
    for j in range(nkb - 1, -1, -1):
        ks = j * SB_KBLK
        c, a = _sb_step(qs, k_ref[0, ks:ks + SB_KBLK, :], v_ref[0, ks:ks + SB_KBLK, :], tri, c, a,
                        (ks + (col & (SB_KBLK - 1))) < (qpos0 + row))
    for p in range(npair):
        o_ref[0, :, p * LANES:(p + 1) * LANES] = a[p].astype(o_ref.dtype)


def _sb_sample(q, k_all, v_all, tri, *, qpos0):
    B, tq, W = q.shape
    Tk = k_all.shape[1]
    qspec = pl.BlockSpec((1, tq, W), lambda b: (b, 0, 0))
    kspec = pl.BlockSpec((1, Tk, W), lambda b: (b, 0, 0))
    return pl.pallas_call(
        functools.partial(_sb_sample_kernel, tq=tq, nkb=Tk // SB_KBLK, qpos0=qpos0),
        out_shape=jax.ShapeDtypeStruct((B, tq, W), BF16), grid=(B,),
        in_specs=[qspec, kspec, kspec, _const_spec(tri.shape)], out_specs=qspec,
        compiler_params=_params(("parallel",)), name="sb_sample")(q, k_all, v_all, tri)


def _mla_prompt_kernel(q_ref, kt_ref, vx_ref, o_ref, s_ref, p_ref, m_ref, a_ref, acc_ref, *, tq, tk, kvlora):
    i = pl.program_id(1)
    H = q_ref.shape[1]
    kc = q_ref.shape[3]
    group_heads = H // MLA_ROW_GROUPS
    gr = group_heads * tq
    m_ref[...] = jnp.full(m_ref.shape, -jnp.inf, F32)
    acc_ref[...] = jnp.zeros(acc_ref.shape, F32)

    def step(j, masked):
        ks = pl.multiple_of(j * tk, tk)
        kt = kt_ref[0, :, pl.ds(ks, tk)]
        vx = vx_ref[0, pl.ds(ks, tk), :]
        for g in range(MLA_ROW_GROUPS):
            qg = q_ref[0, g * group_heads:(g + 1) * group_heads].reshape(gr, kc)
            s_ref[g * gr:(g + 1) * gr, :] = _dot(qg, kt)
        def score_tiles(r0):
            rs = slice(r0, r0 + MLA_ROW_CHUNK)
            ntile = tk // LANES
            if masked:
                ntile = ((r0 % tq) + MLA_ROW_CHUNK - 1) // LANES + 1
            tiles = [s_ref[rs, t * LANES:(t + 1) * LANES] for t in range(ntile)]
            if masked:
                row = lax.broadcasted_iota(jnp.int32, (MLA_ROW_CHUNK, LANES), 0)
                col = lax.broadcasted_iota(jnp.int32, (MLA_ROW_CHUNK, LANES), 1)
                qchunk = (i * tq + (r0 % tq) + row) >> CHUNK_SHIFT
                tiles = [jnp.where(((j * tk + t * LANES + col) >> CHUNK_SHIFT) <= qchunk, st, -jnp.inf)
                         for t, st in enumerate(tiles)]
            return tiles

        for g in range(MLA_ROW_GROUPS):
            for c in range(gr // MLA_ROW_CHUNK):
                r0 = g * gr + c * MLA_ROW_CHUNK
                rs = slice(r0, r0 + MLA_ROW_CHUNK)
                tiles = score_tiles(r0)
                m_old = m_ref[rs, :]
                m_new = jnp.maximum(m_old, jnp.max(functools.reduce(jnp.maximum, tiles), axis=-1, keepdims=True))
                for t in range(len(tiles), tk // LANES):
                    p_ref[rs, t * LANES:(t + 1) * LANES] = jnp.zeros((MLA_ROW_CHUNK, LANES), BF16)
                for t, st in enumerate(tiles):
                    p_ref[rs, t * LANES:(t + 1) * LANES] = jnp.exp(st - m_new).astype(BF16)
                a_ref[rs, :] = jnp.exp(m_old - m_new)
                m_ref[rs, :] = m_new
            gs = slice(g * gr, (g + 1) * gr)
            alpha = a_ref[gs, :]
            acc_ref[gs, :] = jnp.concatenate([alpha, alpha], axis=-1) * acc_ref[gs, :] + _dot(p_ref[gs, :], vx)

    nfull = (i * tq) // tk

    def body(j, c):
        step(j, False)
        return c

    lax.fori_loop(0, nfull, body, 0)
    for d in range(tq // tk):
        step(nfull + d, True)
    for h in range(H):
        hs = slice(h * tq, (h + 1) * tq)
        o_ref[0, :, h * kvlora:(h + 1) * kvlora] = (acc_ref[hs, :kvlora] / acc_ref[hs, kvlora:]).astype(o_ref.dtype)


def _mla_prompt(qcat, kcat_t, vext, *, tq, tk, kvlora):
    B, H, T, KC = qcat.shape
    grid = (B, T // tq)
    rows = H * tq
    return pl.pallas_call(
        functools.partial(_mla_prompt_kernel, tq=tq, tk=tk, kvlora=kvlora),
        out_shape=jax.ShapeDtypeStruct((B, T, H * kvlora), BF16), grid=grid,
        in_specs=[pl.BlockSpec((1, H, tq, KC), lambda b, i: (b, 0, i, 0)), pl.BlockSpec((1, KC, T), lambda b, i: (b, 0, 0)),
                  pl.BlockSpec((1, T, 2 * kvlora), lambda b, i: (b, 0, 0))],
        out_specs=pl.BlockSpec((1, tq, H * kvlora), lambda b, i: (b, i, 0)),
        scratch_shapes=[pltpu.VMEM((rows, tk), F32), pltpu.VMEM((rows, tk), BF16), pltpu.VMEM((rows, LANES), F32),
                        pltpu.VMEM((rows, LANES), F32), pltpu.VMEM((rows, 2 * kvlora), F32)],
        compiler_params=_params(("parallel", "arbitrary")), name="mla_prompt")(qcat, kcat_t, vext)


def _mla_sample_kernel(q_ref, kc_ref, o_ref, *, tq, qpos0, nvalid, kvlora):
    H = q_ref.shape[1]
    q = q_ref[0].reshape(H * tq, q_ref.shape[3])
    kall = kc_ref[0]
    s = _dot_nt(q, kall)
    row = lax.broadcasted_iota(jnp.int32, s.shape, 0)
    col = lax.broadcasted_iota(jnp.int32, s.shape, 1)
    qpos = qpos0 + (row & (tq - 1))
    keep = jnp.logical_and((col >> CHUNK_SHIFT) <= (qpos >> CHUNK_SHIFT), col < nvalid)
    s = jnp.where(keep, s, -jnp.inf)
    p = jnp.exp(s - jnp.max(s, axis=-1, keepdims=True))
    out = _dot(p.astype(BF16), kall[:, :kvlora]) / jnp.sum(p, axis=-1, keepdims=True)
    for h in range(H):
        o_ref[0, :, h * kvlora:(h + 1) * kvlora] = out[h * tq:(h + 1) * tq].astype(o_ref.dtype)


def _mla_sample(qcat, kcat_all, *, qpos0, nvalid, kvlora):
    B, H, tq, KC = qcat.shape
    Tk = kcat_all.shape[1]
    return pl.pallas_call(
        functools.partial(_mla_sample_kernel, tq=tq, qpos0=qpos0, nvalid=nvalid, kvlora=kvlora),
        out_shape=jax.ShapeDtypeStruct((B, tq, H * kvlora), BF16), grid=(B,),
        in_specs=[pl.BlockSpec((1, H, tq, KC), lambda b: (b, 0, 0, 0)), pl.BlockSpec((1, Tk, KC), lambda b: (b, 0, 0))],
        out_specs=pl.BlockSpec((1, tq, H * kvlora), lambda b: (b, 0, 0)),
        compiler_params=_params(("parallel",)), name="mla_sample")(qcat, kcat_all)


def _router_logits(xf_bf, wrh_ref, wrl_ref, br_ref):
    return _dot(xf_bf, wrh_ref[...]) + _dot(xf_bf, wrl_ref[...]) + br_ref[...]


def _route(logits, gidx=None):
    lane = lax.broadcasted_iota(jnp.int32, logits.shape, 1)
    ninf = -jnp.inf
    gl = jnp.where(lane < N_GROUPS, logits, ninf)
    gmax = jnp.max(gl, axis=-1, keepdims=True)
    gexp = jnp.where(lane < N_GROUPS, jnp.exp(logits - gmax), 0.0)
    if gidx is None:
        gidx = jnp.min(jnp.where(gl == gmax, lane, LANES), axis=-1, keepdims=True)
    g_val = jnp.sum(jnp.where(lane == gidx, gexp, 0.0), axis=-1, keepdims=True) / jnp.sum(gexp, axis=-1, keepdims=True)
    lo = ROUTER_LANE0 + EXPERTS_PER_GROUP * gidx
    el = jnp.where(jnp.logical_and(lane >= lo, lane < lo + EXPERTS_PER_GROUP), logits, ninf)
    v1 = jnp.max(el, axis=-1, keepdims=True)
    i1 = jnp.min(jnp.where(el == v1, lane, LANES), axis=-1, keepdims=True)
    el2 = jnp.where(lane == i1, ninf, el)
    v2 = jnp.max(el2, axis=-1, keepdims=True)
    i2 = jnp.min(jnp.where(el2 == v2, lane, LANES), axis=-1, keepdims=True)
    e2 = jnp.exp(v2 - v1)
    den = 1.0 + e2
    return jnp.where(lane == i1, (1.0 / den) * g_val, 0.0) + jnp.where(lane == i2, (e2 / den) * g_val, 0.0), gidx


def _merge_kernel(x_ref, sb_ref, lat_ref, ln1_ref, wg_ref, wuv_ref, wsb_ref, wmla_ref, wout_ref, ln2_ref, wrh_ref, wrl_ref,
                  br_ref, tril_ref, h_ref, xf_ref, comb_ref, info_ref, cnt_ref, *, d):
    @pl.when(pl.program_id(0) == 0)
    def _():
        cnt_ref[...] = jnp.zeros(cnt_ref.shape, F32)

    x = x_ref[...]
    xn = _rms(x, ln1_ref[...]).astype(BF16)
    g = _dot(xn, wg_ref[...])
    mla_o = _dot(lat_ref[...], wuv_ref[...]).astype(BF16)
    merged = jax.nn.sigmoid(g[:, :d]) * _dot(sb_ref[...], wsb_ref[...]) + jax.nn.sigmoid(g[:, d:]) * _dot(mla_o, wmla_ref[...])
    h = x + _dot(merged.astype(BF16), wout_ref[...])
    h_ref[...] = h
    xf = _rms(h, ln2_ref[...]).astype(BF16)
    xf_ref[...] = xf
    comb, gidx = _route(_router_logits(xf, wrh_ref, wrl_ref, br_ref))
    comb_ref[...] = comb
    lane = lax.broadcasted_iota(jnp.int32, comb.shape, 1)
    onehot = lane == gidx
    before = _dot(tril_ref[...], jnp.where(onehot, 1.0, 0.0).astype(BF16)) + cnt_ref[...]
    rank = jnp.sum(jnp.where(onehot, before, 0.0), axis=-1, keepdims=True).astype(jnp.int32)
    info_ref[...] = jnp.where(lane == 0, rank, jnp.where(lane == 1, gidx, 0))
    cnt_ref[...] += jnp.sum(jnp.where(onehot, 1.0, 0.0), axis=0, keepdims=True)


def _merge(x, sb_o, lat, ln1, wg, wuv, wsb, wmla, wout, ln2, wrh, wrl, br, tril, *, tm):
    N, D = x.shape
    tok = lambda w: pl.BlockSpec((tm, w), lambda i: (i, 0))
    consts = (ln1, wg, wuv, wsb, wmla, wout, ln2, wrh, wrl, br, tril)
    return pl.pallas_call(
        functools.partial(_merge_kernel, d=D),
        out_shape=(jax.ShapeDtypeStruct((N, D), F32), jax.ShapeDtypeStruct((N, D), BF16), jax.ShapeDtypeStruct((N, LANES), F32),
                   jax.ShapeDtypeStruct((N, LANES), jnp.int32), jax.ShapeDtypeStruct((1, LANES), F32)),
        grid=(N // tm,), in_specs=[tok(D), tok(sb_o.shape[1]), tok(lat.shape[1])] + [_const_spec(a.shape) for a in consts],
        out_specs=(tok(D), tok(D), tok(LANES), tok(LANES), _const_spec((1, LANES))),
        compiler_params=_params(("arbitrary",)), name="merge")(x, sb_o, lat, *consts)


def _moe_kernel(xf_ref, comb_ref, h_ref, wg_ref, wu_ref, wd_ref, lnf_ref, y_ref, acc_ref, *, final_norm):
    e = pl.program_id(1)

    @pl.when(e == 0)
    def _():
        acc_ref[...] = jnp.zeros(acc_ref.shape, F32)

    xf = xf_ref[...]
    comb = comb_ref[...]
    lane = lax.broadcasted_iota(jnp.int32, comb.shape, 1)
    c = jnp.sum(jnp.where(lane == e + ROUTER_LANE0, comb, 0.0), axis=-1, keepdims=True)
    act = jax.nn.silu(_dot(xf, wg_ref[0])) * _dot(xf, wu_ref[0])
    acc_ref[...] += _dot((act * c).astype(BF16), wd_ref[0])

    @pl.when(e == pl.num_programs(1) - 1)
    def _():
        y = h_ref[...] + acc_ref[...]
        y_ref[...] = _rms(y, lnf_ref[...]) if final_norm else y


def _moe(xf, comb, h, wg, wu, wd, lnf, *, tm, final_norm):
    N, D = h.shape
    E, _, F = wg.shape
    tok = lambda w: pl.BlockSpec((tm, w), lambda i, e: (i, 0))
    return pl.pallas_call(
        functools.partial(_moe_kernel, final_norm=final_norm), out_shape=jax.ShapeDtypeStruct((N, D), F32), grid=(N // tm, E),
        in_specs=[tok(D), tok(LANES), tok(D), pl.BlockSpec((1, D, F), lambda i, e: (e, 0, 0)),
                  pl.BlockSpec((1, D, F), lambda i, e: (e, 0, 0)), pl.BlockSpec((1, F, D), lambda i, e: (e, 0, 0)),
                  pl.BlockSpec((1, D), lambda i, e: (0, 0))],
        out_specs=tok(D), scratch_shapes=[pltpu.VMEM((tm, D), F32)],
        compiler_params=_params(("parallel", "arbitrary")), name="moe")(xf, comb, h, wg, wu, wd, lnf)


def _to_row_tiles(ref, x):
    tiles = x.shape[1] // LANES
    for k in range(tiles):
        ref[pl.ds(k, x.shape[0], stride=tiles), :] = x[:, k * LANES:(k + 1) * LANES]


def _from_row_tiles(ref, tiles):
    n = ref.shape[0] // tiles
    return jnp.concatenate([ref[pl.ds(k, n, stride=tiles), :] for k in range(tiles)], axis=-1)


def _row_tile(ref, r, tiles):
    return ref.at[pl.ds(pl.multiple_of(r * tiles, tiles), tiles)]


def _row_dma_loop(n, make_copy):
    unroll = 8

    def issue(k, c):
        for u in range(unroll):
            make_copy(k * unroll + u).start(priority=u % 2)
        return c

    def drain(r, c):
        make_copy(0).wait()
        return c

    lax.fori_loop(0, n // unroll, issue, 0)
    lax.fori_loop(0, n, drain, 0, unroll=unroll)


def _dispatch_kernel(pos_hbm, xf_ref, init_hbm, out_hbm, idx_ref, buf_ref, sem_idx, sem_rows, *, tm):
    del init_hbm
    idx_copy = pltpu.make_async_copy(pos_hbm.at[pl.program_id(0)], idx_ref, sem_idx)
    idx_copy.start()
    _to_row_tiles(buf_ref, xf_ref[...].astype(F32))
    idx_copy.wait()
    tiles = out_hbm.shape[1]
    _row_dma_loop(tm, lambda r: pltpu.make_async_copy(_row_tile(buf_ref, r, tiles), out_hbm.at[idx_ref[0, r]], sem_rows))


def _dispatch(pos, xf, n_rows, *, tm):
    N, D = xf.shape
    tiles = D // LANES
    init = jnp.zeros((n_rows, tiles, LANES), F32)
    anyspec = pl.BlockSpec(memory_space=pl.ANY)
    return pl.pallas_call(
        functools.partial(_dispatch_kernel, tm=tm), out_shape=jax.ShapeDtypeStruct(init.shape, F32), grid=(N // tm,),
        in_specs=[anyspec, pl.BlockSpec((tm, D), lambda i: (i, 0)), anyspec], out_specs=anyspec,
        scratch_shapes=[pltpu.SMEM((1, tm), jnp.int32), pltpu.VMEM((tm * tiles, LANES), F32), pltpu.SemaphoreType.DMA,
                        pltpu.SemaphoreType.DMA],
        input_output_aliases={2: 0}, compiler_params=_params(("arbitrary",)), name="moe_dispatch")(
        pos.reshape(N // tm, 1, tm), xf, init)


def _moe_grouped_kernel(grp_ref, valid_ref, xs_ref, wg_ref, wu_ref, wd_ref, wrh_ref, wrl_ref, br_ref, ys_ref, *, tiles):
    w = pl.program_id(0)
    grp = grp_ref[w]

    @pl.when(valid_ref[w] == 0)
    def _():
        ys_ref[...] = jnp.zeros(ys_ref.shape, F32)

    @pl.when(valid_ref[w] == 1)
    def _():
        x = _from_row_tiles(xs_ref, tiles).astype(BF16)
        comb = _route(_router_logits(x, wrh_ref, wrl_ref, br_ref), grp)[0]
        lane = lax.broadcasted_iota(jnp.int32, comb.shape, 1)
        lane0 = ROUTER_LANE0 + EXPERTS_PER_GROUP * grp
        out = None
        for e in range(EXPERTS_PER_GROUP):
            c = jnp.sum(jnp.where(lane == lane0 + e, comb, 0.0), axis=-1, keepdims=True)
            act = jax.nn.silu(_dot(x, wg_ref[e])) * _dot(x, wu_ref[e])
            y = _dot((act * c).astype(BF16), wd_ref[e])
            out = y if out is None else out + y
        _to_row_tiles(ys_ref, out)


def _moe_grouped(grp, valid, xs, wg, wu, wd, wrh, wrl, br, *, tm):
    S, tiles, _ = xs.shape
    D = tiles * LANES
    F = wg.shape[2]
    rows = pl.BlockSpec((tm * tiles, LANES), lambda w, grp, valid: (w, 0))
    group = lambda w, grp, valid: (grp[w], 0, 0)
    const = lambda shape: pl.BlockSpec(shape, lambda w, grp, valid: (0,) * len(shape))
    grid_spec = pltpu.PrefetchScalarGridSpec(
        num_scalar_prefetch=2, grid=(S // tm,),
        in_specs=[rows, pl.BlockSpec((EXPERTS_PER_GROUP, D, F), group), pl.BlockSpec((EXPERTS_PER_GROUP, D, F), group),
                  pl.BlockSpec((EXPERTS_PER_GROUP, F, D), group), const(wrh.shape), const(wrl.shape), const(br.shape)],
        out_specs=rows)
    flat = (S * tiles, LANES)
    return pl.pallas_call(functools.partial(_moe_grouped_kernel, tiles=tiles), out_shape=jax.ShapeDtypeStruct(flat, F32),
                          grid_spec=grid_spec, compiler_params=_params(("arbitrary",)), name="moe_grouped")(
        grp, valid, xs.reshape(flat), wg, wu, wd, wrh, wrl, br).reshape(xs.shape)


def _combine_kernel(pos_hbm, h_ref, lnf_ref, ys_hbm, y_ref, idx_ref, buf_ref, sem_idx, sem_rows, *, tm, final_norm):
    idx_copy = pltpu.make_async_copy(pos_hbm.at[pl.program_id(0)], idx_ref, sem_idx)
    idx_copy.start()
    idx_copy.wait()
    tiles = ys_hbm.shape[1]
    _row_dma_loop(tm, lambda r: pltpu.make_async_copy(ys_hbm.at[idx_ref[0, r]], _row_tile(buf_ref, r, tiles), sem_rows))
    y = h_ref[...] + _from_row_tiles(buf_ref, tiles)
    y_ref[...] = _rms(y, lnf_ref[...]) if final_norm else y


def _combine(pos, h, lnf, ys, *, tm, final_norm):
    N, D = h.shape
    tiles = D // LANES
    anyspec = pl.BlockSpec(memory_space=pl.ANY)
    return pl.pallas_call(
        functools.partial(_combine_kernel, tm=tm, final_norm=final_norm), out_shape=jax.ShapeDtypeStruct((N, D), F32),
        grid=(N // tm,), in_specs=[anyspec, pl.BlockSpec((tm, D), lambda i: (i, 0)), _const_spec(lnf.shape), anyspec],
        out_specs=pl.BlockSpec((tm, D), lambda i: (i, 0)),
        scratch_shapes=[pltpu.SMEM((1, tm), jnp.int32), pltpu.VMEM((tm * tiles, LANES), F32), pltpu.SemaphoreType.DMA,
                        pltpu.SemaphoreType.DMA],
        compiler_params=_params(("arbitrary",)), name="moe_combine")(pos.reshape(N // tm, 1, tm), h, lnf, ys)


def _group_layout(info, cnt, *, tm, n_tiles):
    rank, gidx = info[:, 0], info[:, 1]
    counts = cnt[0, :N_GROUPS].astype(jnp.int32)
    ntiles = (counts + tm - 1) // tm
    tile_end = jnp.cumsum(ntiles)
    pos = (tile_end - ntiles)[gidx] * tm + rank
    tile = jnp.arange(n_tiles, dtype=jnp.int32)
    valid = (tile < tile_end[-1]).astype(jnp.int32)
    grp = jnp.sum((jnp.minimum(tile, tile_end[-1] - 1)[:, None] >= tile_end[None, :]).astype(jnp.int32), axis=1)
    return pos, grp, valid


def _pad_lanes(w):
    return jnp.pad(w, [(0, 0)] * (w.ndim - 1) + [(0, LANES - w.shape[-1])])


def _swap_halves(w):
    half = w.shape[-1] // 2
    return jnp.concatenate([w[..., half:], w[..., :half]], axis=-1)


def _rope_tables(pos, rope):
    half = rope // 2
    inv_freq = ROPE_BASE ** (-jnp.arange(half, dtype=F32) / half)
    ang = pos.astype(F32)[:, None] * inv_freq[None, :]
    cos, sin = jnp.cos(ang), jnp.sin(ang)
    return _pad_lanes(jnp.concatenate([cos, cos], axis=-1)), _pad_lanes(jnp.concatenate([-sin, sin], axis=-1))


def _sb_scan_operator():
    n = SB_KBLK
    tri = (jnp.arange(n)[:, None] >= jnp.arange(n)[None, :]).astype(F32)
    z = jnp.zeros((n, n), F32)
    return jnp.concatenate([jnp.concatenate([tri, z], axis=1), jnp.concatenate([z, tri], axis=1)], axis=0).astype(BF16)


def _strict_lower(n):
    return (jnp.arange(n)[:, None] > jnp.arange(n)[None, :]).astype(BF16)


def _pad_rows(a, rows):
    return jnp.pad(a, [(0, 0), (0, rows - a.shape[1]), (0, 0)])


def kernel(x_prompt, x_sample, cache_sb_k, cache_sb_v, cache_mla_ckv, cache_mla_kpe, ln1, w_in, q_norm, w_q_up, kv_norm, w_uk,
           w_uv, w_sb_branch, w_mla_branch, w_out, ln2, w_group, b_group, w_router, b_router, w_gate, w_up, w_down, ln_f):
    B, T, D = x_prompt.shape
    SBt, SQ, _ = x_sample.shape
    depth = ln1.shape[0]
    past = cache_sb_k.shape[2]
    sbw = cache_sb_k.shape[3] * cache_sb_k.shape[4]
    sb_hd = cache_sb_k.shape[4]
    qlora = q_norm.shape[1]
    kvlora = kv_norm.shape[1]
    rope = cache_mla_kpe.shape[3]
    nope = w_uk.shape[3]
    vdim = w_uv.shape[3]
    H = w_uk.shape[2]
    sb_scale = 1.0 / math.sqrt(sb_hd)
    mla_scale = 1.0 / math.sqrt(nope + rope)

    cos_p, sin_p = _rope_tables(jnp.arange(T, dtype=jnp.int32), rope)
    pos_s = past + jnp.arange(SQ, dtype=jnp.int32)
    cos_s, sin_s = _rope_tables(jnp.tile(pos_s, SBt), rope)
    tri = _sb_scan_operator()
    eye = jnp.eye(H, dtype=F32)
    row = lambda a: a.reshape(1, -1)

    tm_p = min(512, T)
    tq_mla = min(512, T)
    ns = SBt * SQ
    tk_all = -(-(past + SQ) // SB_KBLK) * SB_KBLK

    xp = x_prompt
    xs = x_sample.reshape(1, ns, D)
    outs_p = [[], [], [], []]
    outs_s = [[], [], [], []]
    for l in range(depth):
        wl = w_in[l]
        c = 3 * sbw + qlora + kvlora
        wkpe = wl[:, c:c + rope]
        wmain = jnp.concatenate([wl[:, :c], _pad_lanes(wkpe), _pad_lanes(_swap_halves(wkpe))], axis=1).astype(BF16)
        wgates = wl[:, c + rope:].astype(BF16)
        wq = w_q_up[l]
        wqn = wq[:, :, :nope].reshape(qlora, H * nope).astype(BF16)
        wqpe = _pad_lanes(wq[:, :, nope:]).reshape(qlora, H * LANES).astype(BF16)
        wqpes = _pad_lanes(_swap_halves(wq[:, :, nope:])).reshape(qlora, H * LANES).astype(BF16)
        wuk_bd = jnp.einsum('hdc,hg->hdgc', jnp.transpose(w_uk[l], (1, 2, 0)), eye).reshape(H * nope, H * kvlora).astype(BF16)
        wuv_bd = jnp.einsum('hcd,hg->hcgd', jnp.transpose(w_uv[l], (1, 0, 2)), eye).reshape(H * kvlora, H * vdim).astype(BF16)
        wr = _pad_lanes(jnp.concatenate([w_group[l], w_router[l]], axis=1))
        wrh = wr.astype(BF16)
        wrl = (wr - wrh.astype(F32)).astype(BF16)
        br = _pad_lanes(jnp.concatenate([b_group[l], b_router[l]]).reshape(1, -1))
        wsb = w_sb_branch[l].astype(BF16)
        wmla = w_mla_branch[l].astype(BF16)
        wo = w_out[l].astype(BF16)
        wg_e = w_gate[l].astype(BF16)
        wu_e = w_up[l].astype(BF16)
        wd_e = w_down[l].astype(BF16)
        lnf = row(ln_f)
        last = l == depth - 1
        proj_args = (row(ln1[l]), wmain, row(q_norm[l]), row(kv_norm[l]), wqn, wuk_bd, wqpe, wqpes)
        proj_kw = dict(sb_scale=sb_scale, mla_scale=mla_scale, sbw=sbw, qlora=qlora, kvlora=kvlora, rope=rope)
        merge_args = (row(ln1[l]), wgates, wuv_bd, wsb, wmla, wo, row(ln2[l]), wrh, wrl, br)

        def moe_block(h, xf, comb, info, cnt):
            n = h.shape[0]
            if n < MOE_GROUPED_MIN_TOKENS:
                return _moe(xf, comb, h, wg_e, wu_e, wd_e, lnf, tm=n, final_norm=last)
            tm_g = min(MOE_GROUP_TILE, n // N_GROUPS)
            tm_tok = min(MOE_ROW_DMA_TILE, n)
            n_tiles = n // tm_g + N_GROUPS
            pos, grp, valid = _group_layout(info, cnt, tm=tm_g, n_tiles=n_tiles)
            xs_rows = _dispatch(pos, xf, n_tiles * tm_g, tm=tm_tok)
            ys_rows = _moe_grouped(grp, valid, xs_rows, wg_e, wu_e, wd_e, wrh, wrl, br, tm=tm_g)
            return _combine(pos, h, lnf, ys_rows, tm=tm_tok, final_norm=last)

        q, k, v, kb, vb, ckv, kpe, _, kcat_t, vext, qcat = _proj(xp, cos_p, sin_p, *proj_args, tm=tm_p, **proj_kw)
        sb_o = _sb_prompt(q, kb, vb, tri)
        lat = _mla_prompt(qcat, kcat_t, vext, tq=tq_mla, tk=tq_mla, kvlora=kvlora)
        y = moe_block(*_merge(xp.reshape(B * T, D), sb_o.reshape(B * T, sbw), lat.reshape(B * T, H * kvlora), *merge_args,
                              _strict_lower(tm_p), tm=tm_p))
        xp = y.reshape(B, T, D)
        for lst, a in zip(outs_p, (k, v, ckv, kpe)):
            lst.append(a)

        q2, k2, v2, kb2, vb2, ckv2, kpe2, kcat2, _, _, qcat2 = _proj(xs, cos_s, sin_s, *proj_args, tm=ns, **proj_kw)
        per_stream = lambda a: a.reshape(SBt, SQ, a.shape[-1])
        k_all = _pad_rows(jnp.concatenate([cache_sb_k[l].reshape(SBt, past, sbw).astype(BF16), per_stream(kb2)], axis=1), tk_all)
        v_all = _pad_rows(jnp.concatenate([cache_sb_v[l].reshape(SBt, past, sbw).astype(BF16), per_stream(vb2)], axis=1), tk_all)
        kc_past = jnp.concatenate([cache_mla_ckv[l], _pad_lanes(cache_mla_kpe[l])], axis=-1).astype(BF16)
        kcat_all = _pad_rows(jnp.concatenate([kc_past, per_stream(kcat2)], axis=1), tk_all)
        sb_o2 = _sb_sample(per_stream(q2), k_all, v_all, tri, qpos0=past)
        qcat_s = jnp.transpose(qcat2.reshape(H, SBt, SQ, kvlora + LANES), (1, 0, 2, 3))
        lat2 = _mla_sample(qcat_s, kcat_all, qpos0=past, nvalid=past + SQ, kvlora=kvlora)
        y2 = moe_block(*_merge(xs.reshape(ns, D), sb_o2.reshape(ns, sbw), lat2.reshape(ns, H * kvlora), *merge_args,
                              _strict_lower(ns), tm=ns))
        xs = y2.reshape(1, ns, D)
        for lst, a in zip(outs_s, (k2, v2, ckv2, kpe2)):
            lst.append(a)

    heads = lambda a, n: a.reshape(n, -1, SB_HEADS, sb_hd)
    return (xp, xs.reshape(SBt, SQ, D),
            jnp.stack([heads(a, B) for a in outs_p[0]]), jnp.stack([heads(a, B) for a in outs_p[1]]),
            jnp.stack(outs_p[2]), jnp.stack(outs_p[3]),
            jnp.stack([heads(a.reshape(SBt, SQ, sbw), SBt) for a in outs_s[0]]),
            jnp.stack([heads(a.reshape(SBt, SQ, sbw), SBt) for a in outs_s[1]]),
            jnp.stack([a.reshape(SBt, SQ, kvlora) for a in outs_s[2]]),
            jnp.stack([a.reshape(SBt, SQ, rope) for a in outs_s[3]]))
```

```python
import functools
import math

import jax
import jax.numpy as jnp
from jax import lax
from jax.experimental import pallas as pl
from jax.experimental.pallas import tpu as pltpu

F32 = jnp.float32
BF16 = jnp.bfloat16

CHUNK = 64
CHUNK_SHIFT = CHUNK.bit_length() - 1
assert 1 << CHUNK_SHIFT == CHUNK
SB_HEADS = 8
MLA_HEADS = 8
ROPE_BASE = 10000.0
N_GROUPS = 4
EXPERTS_PER_GROUP = 8
N_EXPERTS = N_GROUPS * EXPERTS_PER_GROUP
RMS_EPS = 1e-6

LANES = 128
SB_KBLK = LANES
SB_EAGER_BLOCKS = 2
SB_DEAD_CARRY = -110.0
MLA_ROW_GROUPS = 8
MLA_ROW_CHUNK = 64
VMEM_LIMIT = 56 * 1024 * 1024
ROUTER_LANE0 = N_GROUPS
MOE_GROUP_TILE = 512
MOE_ROW_DMA_TILE = 2048
MOE_GROUPED_MIN_TOKENS = 2048


def _dot(a, b):
    return jnp.dot(a, b, preferred_element_type=F32)


def _dot_nt(a, b):
    return lax.dot_general(a, b, (((1,), (1,)), ((), ())), preferred_element_type=F32)


def _rms(x, g):
    return x * lax.rsqrt(jnp.mean(x * x, axis=-1, keepdims=True) + RMS_EPS) * g


def _params(sem):
    return pltpu.CompilerParams(dimension_semantics=sem, vmem_limit_bytes=VMEM_LIMIT)


def _const_spec(shape):
    return pl.BlockSpec(shape, lambda *_: (0,) * len(shape))


def _proj_kernel(x_ref, cos_ref, sin_ref, ln1_ref, wmain_ref, qn_ref, kvn_ref, wqn_ref, wuk_ref, wqpe_ref, wqpes_ref,
                 q_ref, k_ref, v_ref, kb_ref, vb_ref, ckv_ref, kpe_ref, kcat_ref, kcatt_ref, vext_ref, qcat_ref, *, sb_scale, mla_scale, sbw,
                 qlora, kvlora, rope):
    xn = _rms(x_ref[0], ln1_ref[...]).astype(BF16)
    p = _dot(xn, wmain_ref[...])
    o = 0
    q = p[:, o:o + sbw]; o += sbw
    k = p[:, o:o + sbw]; o += sbw
    v = p[:, o:o + sbw]; o += sbw
    cq = p[:, o:o + qlora]; o += qlora
    ckv = p[:, o:o + kvlora]; o += kvlora
    kpe = p[:, o:o + LANES]; o += LANES
    kpe_sw = p[:, o:o + LANES]
    q_ref[0] = (q * sb_scale).astype(BF16)
    k_ref[0] = k
    v_ref[0] = v
    kb_ref[0] = k.astype(BF16)
    vb_ref[0] = v.astype(BF16)
    cos = cos_ref[...]
    sin = sin_ref[...]
    ckvn = _rms(ckv, kvn_ref[...])
    ckv_ref[0] = ckvn
    kpe_rot = kpe * cos + kpe_sw * sin
    kpe_ref[0] = kpe_rot[:, :rope]
    kcat = jnp.concatenate([ckvn, kpe_rot], axis=-1)
    kcat_ref[0] = kcat.astype(BF16)
    kcatt_ref[0] = kcat.T.astype(BF16)
    vext_ref[0] = jnp.concatenate([ckvn, jnp.ones_like(ckvn)], axis=-1).astype(BF16)
    cqn = _rms(cq, qn_ref[...]).astype(BF16)
    qnope = _dot(cqn, wqn_ref[...]).astype(BF16)
    qlat = _dot(qnope, wuk_ref[...])
    qpe = _dot(cqn, wqpe_ref[...])
    qpe_sw = _dot(cqn, wqpes_ref[...])
    for h in range(MLA_HEADS):
        sl = slice(h * LANES, (h + 1) * LANES)
        qcat_ref[0, h, :, :kvlora] = (qlat[:, h * kvlora:(h + 1) * kvlora] * mla_scale).astype(BF16)
        qcat_ref[0, h, :, kvlora:] = ((qpe[:, sl] * cos + qpe_sw[:, sl] * sin) * mla_scale).astype(BF16)


def _proj(x, cos_t, sin_t, ln1, wmain, qn, kvn, wqn, wuk, wqpe, wqpes, *, tm, sb_scale, mla_scale, sbw, qlora, kvlora, rope):
    B, T, D = x.shape
    grid = (B, T // tm)
    tok = lambda w: pl.BlockSpec((1, tm, w), lambda b, i: (b, i, 0))
    tab = pl.BlockSpec((tm, LANES), lambda b, i: (i, 0))
    kcw = kvlora + LANES
    out_shape = (
        jax.ShapeDtypeStruct((B, T, sbw), BF16),
        jax.ShapeDtypeStruct((B, T, sbw), F32),
        jax.ShapeDtypeStruct((B, T, sbw), F32),
        jax.ShapeDtypeStruct((B, T, sbw), BF16),
        jax.ShapeDtypeStruct((B, T, sbw), BF16),
        jax.ShapeDtypeStruct((B, T, kvlora), F32),
        jax.ShapeDtypeStruct((B, T, rope), F32),
        jax.ShapeDtypeStruct((B, T, kcw), BF16),
        jax.ShapeDtypeStruct((B, kcw, T), BF16),
        jax.ShapeDtypeStruct((B, T, 2 * kvlora), BF16),
        jax.ShapeDtypeStruct((B, MLA_HEADS, T, kcw), BF16),
    )
    out_specs = (tok(sbw), tok(sbw), tok(sbw), tok(sbw), tok(sbw), tok(kvlora), tok(rope), tok(kcw),
                 pl.BlockSpec((1, kcw, tm), lambda b, i: (b, 0, i)), tok(2 * kvlora),
                 pl.BlockSpec((1, MLA_HEADS, tm, kcw), lambda b, i: (b, 0, i, 0)))
    in_specs = [tok(D), tab, tab] + [_const_spec(a.shape) for a in (ln1, wmain, qn, kvn, wqn, wuk, wqpe, wqpes)]
    kern = functools.partial(_proj_kernel, sb_scale=sb_scale, mla_scale=mla_scale, sbw=sbw, qlora=qlora, kvlora=kvlora,
                             rope=rope)
    return pl.pallas_call(kern, out_shape=out_shape, grid=grid, in_specs=in_specs, out_specs=out_specs,
                          compiler_params=_params(("parallel", "parallel")), name="proj")(
        x, cos_t, sin_t, ln1, wmain, qn, kvn, wqn, wuk, wqpe, wqpes)


def _head_pair_blockdiag(blk):
    lane = lax.broadcasted_iota(jnp.int32, blk.shape, 1)
    zero = jnp.zeros_like(blk)
    half = LANES // 2
    return jnp.concatenate([jnp.where(lane < half, blk, zero), jnp.where(lane >= half, blk, zero)], axis=0)


def _sb_step(qs, kblk, vblk, tri, carries, accs, keep):
    tq = qs[0].shape[0]
    lbs, loms, parts = [], [], []
    for p, q in enumerate(qs):
        z = _dot_nt(q, _head_pair_blockdiag(kblk[:, p * LANES:(p + 1) * LANES]))
        lb = jnp.minimum(z, 0.0) - jnp.log(1.0 + jnp.exp(-jnp.abs(z)))
        lom = lb - z
        if keep is not None:
            lom = jnp.where(keep, lom, 0.0)
        hi = lom.astype(BF16)
        lbs.append(lb)
        loms.append(lom)
        parts += [hi, (lom - hi.astype(F32)).astype(BF16)]
    r = _dot(jnp.concatenate(parts, axis=0), tri)
    new_c, new_a = [], []
    for p in range(len(qs)):
        incl = r[2 * p * tq:(2 * p + 1) * tq] + r[(2 * p + 1) * tq:(2 * p + 2) * tq]
        w = jnp.exp(lbs[p] + (incl - loms[p]) + carries[p])
        if keep is not None:
            w = jnp.where(keep, w, 0.0)
        new_a.append(accs[p] + _dot(w.astype(BF16), _head_pair_blockdiag(vblk[:, p * LANES:(p + 1) * LANES])))
        tot = jnp.concatenate([jnp.broadcast_to(incl[:, 0:1], (tq, SB_KBLK)),
                               jnp.broadcast_to(incl[:, SB_KBLK:SB_KBLK + 1], (tq, SB_KBLK))], axis=1)
        new_c.append(carries[p] + tot)
    return new_c, new_a


def _sb_prompt_kernel(q_ref, k_ref, v_ref, tri_ref, o_ref, carry_ref, acc_ref, *, tq):
    i = pl.program_id(1)
    npair = q_ref.shape[2] // LANES
    qs = [q_ref[0, :, p * LANES:(p + 1) * LANES] for p in range(npair)]
    tri = tri_ref[...]
    row = lax.broadcasted_iota(jnp.int32, (tq, 2 * SB_KBLK), 0)
    col = lax.broadcasted_iota(jnp.int32, (tq, 2 * SB_KBLK), 1)
    ks = pl.multiple_of(i * SB_KBLK, SB_KBLK)
    zeros_c = [jnp.zeros((tq, 2 * SB_KBLK), F32)] * npair
    zeros_a = [jnp.zeros((tq, LANES), F32)] * npair
    c, a = _sb_step(qs, k_ref[0, pl.ds(ks, SB_KBLK), :], v_ref[0, pl.ds(ks, SB_KBLK), :], tri, zeros_c, zeros_a,
                    (col & (SB_KBLK - 1)) < row)
    for d in range(1, SB_EAGER_BLOCKS + 1):
        kp = pl.multiple_of(jnp.maximum(i - d, 0) * SB_KBLK, SB_KBLK)
        vprev = v_ref[0, pl.ds(kp, SB_KBLK), :]
        c, a = _sb_step(qs, k_ref[0, pl.ds(kp, SB_KBLK), :], jnp.where(i >= d, vprev, jnp.zeros_like(vprev)), tri, c, a,
                        None)
    for p in range(npair):
        carry_ref[p] = c[p]
        acc_ref[p] = a[p]

    def cond(s):
        j, live = s
        return jnp.logical_and(j >= 0, live > SB_DEAD_CARRY)

    def body(s):
        j, _ = s
        ks = pl.multiple_of(j * SB_KBLK, SB_KBLK)
        c, a = _sb_step(qs, k_ref[0, pl.ds(ks, SB_KBLK), :], v_ref[0, pl.ds(ks, SB_KBLK), :], tri,
                        [carry_ref[p] for p in range(npair)], [acc_ref[p] for p in range(npair)], None)
        for p in range(npair):
            carry_ref[p] = c[p]
            acc_ref[p] = a[p]
        return j - 1, jnp.max(functools.reduce(jnp.maximum, c))

    lax.while_loop(cond, body, (i - 1 - SB_EAGER_BLOCKS, jnp.max(functools.reduce(jnp.maximum, c))))
    for p in range(npair):
        o_ref[0, :, p * LANES:(p + 1) * LANES] = acc_ref[p].astype(o_ref.dtype)


def _sb_prompt(q, kb, vb, tri):
    B, T, W = q.shape
    tq = SB_KBLK
    npair = W // LANES
    qspec = pl.BlockSpec((1, tq, W), lambda b, i: (b, i, 0))
    kspec = pl.BlockSpec((1, T, W), lambda b, i: (b, 0, 0))
    return pl.pallas_call(
        functools.partial(_sb_prompt_kernel, tq=tq), out_shape=jax.ShapeDtypeStruct((B, T, W), BF16), grid=(B, T // tq),
        in_specs=[qspec, kspec, kspec, _const_spec(tri.shape)], out_specs=qspec,
        scratch_shapes=[pltpu.VMEM((npair, tq, 2 * SB_KBLK), F32), pltpu.VMEM((npair, tq, LANES), F32)],
        compiler_params=_params(("parallel", "arbitrary")), name="sb_prompt")(q, kb, vb, tri)


def _sb_sample_kernel(q_ref, k_ref, v_ref, tri_ref, o_ref, *, tq, nkb, qpos0):
    nstream = q_ref.shape[0]
    npair = q_ref.shape[2] // LANES
    tri = tri_ref[...]
    row = lax.broadcasted_iota(jnp.int32, (tq, 2 * SB_KBLK), 0)
    col = lax.broadcasted_iota(jnp.int32, (tq, 2 * SB_KBLK), 1)
    qs = [[q_ref[s, :, p * LANES:(p + 1) * LANES] for p in range(npair)] for s in range(nstream)]
    c = [[jnp.zeros((tq, 2 * SB_KBLK), F32)] * npair for _ in range(nstream)]
    a = [[jnp.zeros((tq, LANES), F32)] * npair for _ in range(nstream)]
    for j in range(nkb - 1, -1, -1):
        ks = j * SB_KBLK
        keep = None if ks + SB_KBLK <= qpos0 else (ks + (col & (SB_KBLK - 1))) < (qpos0 + row)
        for s in range(nstream):
            c[s], a[s] = _sb_step(qs[s], k_ref[s, ks:ks + SB_KBLK, :], v_ref[s, ks:ks + SB_KBLK, :], tri, c[s], a[s], keep)
    for s in range(nstream):
        for p in range(npair):
            o_ref[s, :, p * LANES:(p + 1) * LANES] = a[s][p].astype(o_ref.dtype)


def _sb_sample(q, k_all, v_all, tri, *, qpos0):
    B, tq, W = q.shape
    Tk = k_all.shape[1]
    nstream = 2 if B % 2 == 0 else 1
    qspec = pl.BlockSpec((nstream, tq, W), lambda b: (b, 0, 0))
    kspec = pl.BlockSpec((nstream, Tk, W), lambda b: (b, 0, 0))
    return pl.pallas_call(
        functools.partial(_sb_sample_kernel, tq=tq, nkb=Tk // SB_KBLK, qpos0=qpos0),
        out_shape=jax.ShapeDtypeStruct((B, tq, W), BF16), grid=(B // nstream,),
        in_specs=[qspec, kspec, kspec, _const_spec(tri.shape)], out_specs=qspec,
        compiler_params=_params(("parallel",)), name="sb_sample")(q, k_all, v_all, tri)


def _mla_prompt_kernel(q_ref, kt_ref, vx_ref, o_ref, s_ref, p_ref, m_ref, a_ref, acc_ref, *, tq, tk, kvlora):
    i = pl.program_id(1)
    H = q_ref.shape[1]
    kc = q_ref.shape[3]
    group_heads = H // MLA_ROW_GROUPS
    gr = group_heads * tq
    m_ref[...] = jnp.full(m_ref.shape, -jnp.inf, F32)
    acc_ref[...] = jnp.zeros(acc_ref.shape, F32)

    def step(j, masked):
        ks = pl.multiple_of(j * tk, tk)
        kt = kt_ref[0, :, pl.ds(ks, tk)]
        vx = vx_ref[0, pl.ds(ks, tk), :]
        for g in range(MLA_ROW_GROUPS):
            qg = q_ref[0, g * group_heads:(g + 1) * group_heads].reshape(gr, kc)
            s_ref[g * gr:(g + 1) * gr, :] = _dot(qg, kt)
        def score_tiles(r0):
            rs = slice(r0, r0 + MLA_ROW_CHUNK)
            ntile = tk // LANES
            if masked:
                ntile = ((r0 % tq) + MLA_ROW_CHUNK - 1) // LANES + 1
            tiles = [s_ref[rs, t * LANES:(t + 1) * LANES] for t in range(ntile)]
            if masked:
                row = lax.broadcasted_iota(jnp.int32, (MLA_ROW_CHUNK, LANES), 0)
                col = lax.broadcasted_iota(jnp.int32, (MLA_ROW_CHUNK, LANES), 1)
                qchunk = (i * tq + (r0 % tq) + row) >> CHUNK_SHIFT
                tiles = [jnp.where(((j * tk + t * LANES + col) >> CHUNK_SHIFT) <= qchunk, st, -jnp.inf)
                         for t, st in enumerate(tiles)]
            return tiles

        for g in range(MLA_ROW_GROUPS):
            for c in range(gr // MLA_ROW_CHUNK):
                r0 = g * gr + c * MLA_ROW_CHUNK
                rs = slice(r0, r0 + MLA_ROW_CHUNK)
                tiles = score_tiles(r0)
                m_old = m_ref[rs, :]
                m_new = jnp.maximum(m_old, jnp.max(functools.reduce(jnp.maximum, tiles), axis=-1, keepdims=True))
                for t in range(len(tiles), tk // LANES):
                    p_ref[rs, t * LANES:(t + 1) * LANES] = jnp.zeros((MLA_ROW_CHUNK, LANES), BF16)
                for t, st in enumerate(tiles):
                    p_ref[rs, t * LANES:(t + 1) * LANES] = jnp.exp(st - m_new).astype(BF16)
                a_ref[rs, :] = jnp.exp(m_old - m_new)
                m_ref[rs, :] = m_new
            gs = slice(g * gr, (g + 1) * gr)
            alpha = a_ref[gs, :]
            acc_ref[gs, :] = jnp.concatenate([alpha, alpha], axis=-1) * acc_ref[gs, :] + _dot(p_ref[gs, :], vx)

    nfull = (i * tq) // tk

    def body(j, c):
        step(j, False)
        return c

    lax.fori_loop(0, nfull, body, 0)
    for d in range(tq // tk):
        step(nfull + d, True)
    for h in range(H):
        hs = slice(h * tq, (h + 1) * tq)
        o_ref[0, :, h * kvlora:(h + 1) * kvlora] = (acc_ref[hs, :kvlora] / acc_ref[hs, kvlora:]).astype(o_ref.dtype)


def _mla_prompt(qcat, kcat_t, vext, *, tq, tk, kvlora):
    B, H, T, KC = qcat.shape
    grid = (B, T // tq)
    rows = H * tq
    return pl.pallas_call(
        functools.partial(_mla_prompt_kernel, tq=tq, tk=tk, kvlora=kvlora),
        out_shape=jax.ShapeDtypeStruct((B, T, H * kvlora), BF16), grid=grid,
        in_specs=[pl.BlockSpec((1, H, tq, KC), lambda b, i: (b, 0, i, 0)), pl.BlockSpec((1, KC, T), lambda b, i: (b, 0, 0)),
                  pl.BlockSpec((1, T, 2 * kvlora), lambda b, i: (b, 0, 0))],
        out_specs=pl.BlockSpec((1, tq, H * kvlora), lambda b, i: (b, i, 0)),
        scratch_shapes=[pltpu.VMEM((rows, tk), F32), pltpu.VMEM((rows, tk), BF16), pltpu.VMEM((rows, LANES), F32),
                        pltpu.VMEM((rows, LANES), F32), pltpu.VMEM((rows, 2 * kvlora), F32)],
        compiler_params=_params(("parallel", "arbitrary")), name="mla_prompt")(qcat, kcat_t, vext)


def _mla_sample_kernel(q_ref, kc_ref, o_ref, *, tq, qpos0, nvalid, kvlora):
    H = q_ref.shape[1]
    q = q_ref[0].reshape(H * tq, q_ref.shape[3])
    kall = kc_ref[0]
    s = _dot_nt(q, kall)
    row = lax.broadcasted_iota(jnp.int32, s.shape, 0)
    col = lax.broadcasted_iota(jnp.int32, s.shape, 1)
    qpos = qpos0 + (row & (tq - 1))
    keep = jnp.logical_and((col >> CHUNK_SHIFT) <= (qpos >> CHUNK_SHIFT), col < nvalid)
    s = jnp.where(keep, s, -jnp.inf)
    p = jnp.exp(s - jnp.max(s, axis=-1, keepdims=True))
    out = _dot(p.astype(BF16), kall[:, :kvlora]) / jnp.sum(p, axis=-1, keepdims=True)
    for h in range(H):
        o_ref[0, :, h * kvlora:(h + 1) * kvlora] = out[h * tq:(h + 1) * tq].astype(o_ref.dtype)


def _mla_sample(qcat, kcat_all, *, qpos0, nvalid, kvlora):
    B, H, tq, KC = qcat.shape
    Tk = kcat_all.shape[1]
    return pl.pallas_call(
        functools.partial(_mla_sample_kernel, tq=tq, qpos0=qpos0, nvalid=nvalid, kvlora=kvlora),
        out_shape=jax.ShapeDtypeStruct((B, tq, H * kvlora), BF16), grid=(B,),
        in_specs=[pl.BlockSpec((1, H, tq, KC), lambda b: (b, 0, 0, 0)), pl.BlockSpec((1, Tk, KC), lambda b: (b, 0, 0))],
        out_specs=pl.BlockSpec((1, tq, H * kvlora), lambda b: (b, 0, 0)),
        compiler_params=_params(("parallel",)), name="mla_sample")(qcat, kcat_all)


def _router_logits(xf_bf, wrh_ref, wrl_ref, br_ref):
    return _dot(xf_bf, wrh_ref[...]) + _dot(xf_bf, wrl_ref[...]) + br_ref[...]


def _route(logits, gidx=None):
    lane = lax.broadcasted_iota(jnp.int32, logits.shape, 1)
    ninf = -jnp.inf
    gl = jnp.where(lane < N_GROUPS, logits, ninf)
    gmax = jnp.max(gl, axis=-1, keepdims=True)
    gexp = jnp.where(lane < N_GROUPS, jnp.exp(logits - gmax), 0.0)
    if gidx is None:
        gidx = jnp.min(jnp.where(gl == gmax, lane, LANES), axis=-1, keepdims=True)
    g_val = jnp.sum(jnp.where(lane == gidx, gexp, 0.0), axis=-1, keepdims=True) / jnp.sum(gexp, axis=-1, keepdims=True)
    lo = ROUTER_LANE0 + EXPERTS_PER_GROUP * gidx
    el = jnp.where(jnp.logical_and(lane >= lo, lane < lo + EXPERTS_PER_GROUP), logits, ninf)
    v1 = jnp.max(el, axis=-1, keepdims=True)
    i1 = jnp.min(jnp.where(el == v1, lane, LANES), axis=-1, keepdims=True)
    el2 = jnp.where(lane == i1, ninf, el)
    v2 = jnp.max(el2, axis=-1, keepdims=True)
    i2 = jnp.min(jnp.where(el2 == v2, lane, LANES), axis=-1, keepdims=True)
    e2 = jnp.exp(v2 - v1)
    den = 1.0 + e2
    return jnp.where(lane == i1, (1.0 / den) * g_val, 0.0) + jnp.where(lane == i2, (e2 / den) * g_val, 0.0), gidx


def _merge_kernel(x_ref, sb_ref, lat_ref, ln1_ref, wg_ref, wuv_ref, wsb_ref, wmla_ref, wout_ref, ln2_ref, wrh_ref, wrl_ref,
                  br_ref, tril_ref, h_ref, xf_ref, comb_ref, info_ref, cnt_ref, *, d):
    @pl.when(pl.program_id(0) == 0)
    def _():
        cnt_ref[...] = jnp.zeros(cnt_ref.shape, F32)

    x = x_ref[...]
    xn = _rms(x, ln1_ref[...]).astype(BF16)
    g = _dot(xn, wg_ref[...])
    mla_o = _dot(lat_ref[...], wuv_ref[...]).astype(BF16)
    merged = jax.nn.sigmoid(g[:, :d]) * _dot(sb_ref[...], wsb_ref[...]) + jax.nn.sigmoid(g[:, d:]) * _dot(mla_o, wmla_ref[...])
    h = x + _dot(merged.astype(BF16), wout_ref[...])
    h_ref[...] = h
    xf = _rms(h, ln2_ref[...]).astype(BF16)
    xf_ref[...] = xf
    comb, gidx = _route(_router_logits(xf, wrh_ref, wrl_ref, br_ref))
    comb_ref[...] = comb
    lane = lax.broadcasted_iota(jnp.int32, comb.shape, 1)
    onehot = lane == gidx
    before = _dot(tril_ref[...], jnp.where(onehot, 1.0, 0.0).astype(BF16)) + cnt_ref[...]
    rank = jnp.sum(jnp.where(onehot, before, 0.0), axis=-1, keepdims=True).astype(jnp.int32)
    info_ref[...] = jnp.where(lane == 0, rank, jnp.where(lane == 1, gidx, 0))
    cnt_ref[...] += jnp.sum(jnp.where(onehot, 1.0, 0.0), axis=0, keepdims=True)


def _merge(x, sb_o, lat, ln1, wg, wuv, wsb, wmla, wout, ln2, wrh, wrl, br, tril, *, tm):
    N, D = x.shape
    tok = lambda w: pl.BlockSpec((tm, w), lambda i: (i, 0))
    consts = (ln1, wg, wuv, wsb, wmla, wout, ln2, wrh, wrl, br, tril)
    return pl.pallas_call(
        functools.partial(_merge_kernel, d=D),
        out_shape=(jax.ShapeDtypeStruct((N, D), F32), jax.ShapeDtypeStruct((N, D), BF16), jax.ShapeDtypeStruct((N, LANES), F32),
                   jax.ShapeDtypeStruct((N, LANES), jnp.int32), jax.ShapeDtypeStruct((1, LANES), F32)),
        grid=(N // tm,), in_specs=[tok(D), tok(sb_o.shape[1]), tok(lat.shape[1])] + [_const_spec(a.shape) for a in consts],
        out_specs=(tok(D), tok(D), tok(LANES), tok(LANES), _const_spec((1, LANES))),
        compiler_params=_params(("arbitrary",)), name="merge")(x, sb_o, lat, *consts)


def _moe_kernel(xf_ref, comb_ref, h_ref, wg_ref, wu_ref, wd_ref, lnf_ref, y_ref, acc_ref, *, final_norm):
    e = pl.program_id(1)

    @pl.when(e == 0)
    def _():
        acc_ref[...] = jnp.zeros(acc_ref.shape, F32)

    xf = xf_ref[...]
    comb = comb_ref[...]
    lane = lax.broadcasted_iota(jnp.int32, comb.shape, 1)
    c = jnp.sum(jnp.where(lane == e + ROUTER_LANE0, comb, 0.0), axis=-1, keepdims=True)
    act = jax.nn.silu(_dot(xf, wg_ref[0])) * _dot(xf, wu_ref[0])
    acc_ref[...] += _dot((act * c).astype(BF16), wd_ref[0])

    @pl.when(e == pl.num_programs(1) - 1)
    def _():
        y = h_ref[...] + acc_ref[...]
        y_ref[...] = _rms(y, lnf_ref[...]) if final_norm else y


def _moe(xf, comb, h, wg, wu, wd, lnf, *, tm, final_norm):
    N, D = h.shape
    E, _, F = wg.shape
    tok = lambda w: pl.BlockSpec((tm, w), lambda i, e: (i, 0))
    return pl.pallas_call(
        functools.partial(_moe_kernel, final_norm=final_norm), out_shape=jax.ShapeDtypeStruct((N, D), F32), grid=(N // tm, E),
        in_specs=[tok(D), tok(LANES), tok(D), pl.BlockSpec((1, D, F), lambda i, e: (e, 0, 0)),
                  pl.BlockSpec((1, D, F), lambda i, e: (e, 0, 0)), pl.BlockSpec((1, F, D), lambda i, e: (e, 0, 0)),
                  pl.BlockSpec((1, D), lambda i, e: (0, 0))],
        out_specs=tok(D), scratch_shapes=[pltpu.VMEM((tm, D), F32)],
        compiler_params=_params(("parallel", "arbitrary")), name="moe")(xf, comb, h, wg, wu, wd, lnf)


def _to_row_tiles(ref, x):
    tiles = x.shape[1] // LANES
    for k in range(tiles):
        ref[pl.ds(k, x.shape[0], stride=tiles), :] = x[:, k * LANES:(k + 1) * LANES]


def _from_row_tiles(ref, tiles):
    n = ref.shape[0] // tiles
    return jnp.concatenate([ref[pl.ds(k, n, stride=tiles), :] for k in range(tiles)], axis=-1)


def _row_tile(ref, r, tiles):
    return ref.at[pl.ds(pl.multiple_of(r * tiles, tiles), tiles)]


def _row_dma_loop(n, make_copy):
    unroll = 8

    def issue(k, c):
        for u in range(unroll):
            make_copy(k * unroll + u).start(priority=u % 2)
        return c

    def drain(r, c):
        make_copy(0).wait()
        return c

    lax.fori_loop(0, n // unroll, issue, 0)
    lax.fori_loop(0, n, drain, 0, unroll=unroll)


def _dispatch_kernel(pos_hbm, xf_ref, init_hbm, out_hbm, idx_ref, buf_ref, sem_idx, sem_rows, *, tm):
    del init_hbm
    idx_copy = pltpu.make_async_copy(pos_hbm.at[pl.program_id(0)], idx_ref, sem_idx)
    idx_copy.start()
    _to_row_tiles(buf_ref, xf_ref[...].astype(F32))
    idx_copy.wait()
    tiles = out_hbm.shape[1]
    _row_dma_loop(tm, lambda r: pltpu.make_async_copy(_row_tile(buf_ref, r, tiles), out_hbm.at[idx_ref[0, r]], sem_rows))


def _dispatch(pos, xf, n_rows, *, tm):
    N, D = xf.shape
    tiles = D // LANES
    init = jnp.zeros((n_rows, tiles, LANES), F32)
    anyspec = pl.BlockSpec(memory_space=pl.ANY)
    return pl.pallas_call(
        functools.partial(_dispatch_kernel, tm=tm), out_shape=jax.ShapeDtypeStruct(init.shape, F32), grid=(N // tm,),
        in_specs=[anyspec, pl.BlockSpec((tm, D), lambda i: (i, 0)), anyspec], out_specs=anyspec,
        scratch_shapes=[pltpu.SMEM((1, tm), jnp.int32), pltpu.VMEM((tm * tiles, LANES), F32), pltpu.SemaphoreType.DMA,
                        pltpu.SemaphoreType.DMA],
        input_output_aliases={2: 0}, compiler_params=_params(("arbitrary",)), name="moe_dispatch")(
        pos.reshape(N // tm, 1, tm), xf, init)


def _moe_grouped_kernel(grp_ref, valid_ref, xs_ref, wg_ref, wu_ref, wd_ref, wrh_ref, wrl_ref, br_ref, ys_ref, *, tiles):
    w = pl.program_id(0)
    grp = grp_ref[w]

    @pl.when(valid_ref[w] == 0)
    def _():
        ys_ref[...] = jnp.zeros(ys_ref.shape, F32)

    @pl.when(valid_ref[w] == 1)
    def _():
        x = _from_row_tiles(xs_ref, tiles).astype(BF16)
        comb = _route(_router_logits(x, wrh_ref, wrl_ref, br_ref), grp)[0]
        lane = lax.broadcasted_iota(jnp.int32, comb.shape, 1)
        lane0 = ROUTER_LANE0 + EXPERTS_PER_GROUP * grp
        out = None
        for e in range(EXPERTS_PER_GROUP):
            c = jnp.sum(jnp.where(lane == lane0 + e, comb, 0.0), axis=-1, keepdims=True)
            act = jax.nn.silu(_dot(x, wg_ref[e])) * _dot(x, wu_ref[e])
            y = _dot((act * c).astype(BF16), wd_ref[e])
            out = y if out is None else out + y
        _to_row_tiles(ys_ref, out)


def _moe_grouped(grp, valid, xs, wg, wu, wd, wrh, wrl, br, *, tm):
    S, tiles, _ = xs.shape
    D = tiles * LANES
    F = wg.shape[2]
    rows = pl.BlockSpec((tm * tiles, LANES), lambda w, grp, valid: (w, 0))
    group = lambda w, grp, valid: (grp[w], 0, 0)
    const = lambda shape: pl.BlockSpec(shape, lambda w, grp, valid: (0,) * len(shape))
    grid_spec = pltpu.PrefetchScalarGridSpec(
        num_scalar_prefetch=2, grid=(S // tm,),
        in_specs=[rows, pl.BlockSpec((EXPERTS_PER_GROUP, D, F), group), pl.BlockSpec((EXPERTS_PER_GROUP, D, F), group),
                  pl.BlockSpec((EXPERTS_PER_GROUP, F, D), group), const(wrh.shape), const(wrl.shape), const(br.shape)],
        out_specs=rows)
    flat = (S * tiles, LANES)
    return pl.pallas_call(functools.partial(_moe_grouped_kernel, tiles=tiles), out_shape=jax.ShapeDtypeStruct(flat, F32),
                          grid_spec=grid_spec, compiler_params=_params(("arbitrary",)), name="moe_grouped")(
        grp, valid, xs.reshape(flat), wg, wu, wd, wrh, wrl, br).reshape(xs.shape)


def _combine_kernel(pos_hbm, h_ref, lnf_ref, ys_hbm, y_ref, idx_ref, buf_ref, sem_idx, sem_rows, *, tm, final_norm):
    idx_copy = pltpu.make_async_copy(pos_hbm.at[pl.program_id(0)], idx_ref, sem_idx)
    idx_copy.start()
    idx_copy.wait()
    tiles = ys_hbm.shape[1]
    _row_dma_loop(tm, lambda r: pltpu.make_async_copy(ys_hbm.at[idx_ref[0, r]], _row_tile(buf_ref, r, tiles), sem_rows))
    y = h_ref[...] + _from_row_tiles(buf_ref, tiles)
    y_ref[...] = _rms(y, lnf_ref[...]) if final_norm else y


def _combine(pos, h, lnf, ys, *, tm, final_norm):
    N, D = h.shape
    tiles = D // LANES
    anyspec = pl.BlockSpec(memory_space=pl.ANY)
    return pl.pallas_call(
        functools.partial(_combine_kernel, tm=tm, final_norm=final_norm), out_shape=jax.ShapeDtypeStruct((N, D), F32),
        grid=(N // tm,), in_specs=[anyspec, pl.BlockSpec((tm, D), lambda i: (i, 0)), _const_spec(lnf.shape), anyspec],
        out_specs=pl.BlockSpec((tm, D), lambda i: (i, 0)),
        scratch_shapes=[pltpu.SMEM((1, tm), jnp.int32), pltpu.VMEM((tm * tiles, LANES), F32), pltpu.SemaphoreType.DMA,
                        pltpu.SemaphoreType.DMA],
        compiler_params=_params(("arbitrary",)), name="moe_combine")(pos.reshape(N // tm, 1, tm), h, lnf, ys)


def _group_layout(info, cnt, *, tm, n_tiles):
    rank, gidx = info[:, 0], info[:, 1]
    counts = cnt[0, :N_GROUPS].astype(jnp.int32)
    ntiles = (counts + tm - 1) // tm
    tile_end = jnp.cumsum(ntiles)
    pos = (tile_end - ntiles)[gidx] * tm + rank
    tile = jnp.arange(n_tiles, dtype=jnp.int32)
    valid = (tile < tile_end[-1]).astype(jnp.int32)
    grp = jnp.sum((jnp.minimum(tile, tile_end[-1] - 1)[:, None] >= tile_end[None, :]).astype(jnp.int32), axis=1)
    return pos, grp, valid


def _pad_lanes(w):
    return jnp.pad(w, [(0, 0)] * (w.ndim - 1) + [(0, LANES - w.shape[-1])])


def _swap_halves(w):
    half = w.shape[-1] // 2
    return jnp.concatenate([w[..., half:], w[..., :half]], axis=-1)


def _rope_tables(pos, rope):
    half = rope // 2
    inv_freq = ROPE_BASE ** (-jnp.arange(half, dtype=F32) / half)
    ang = pos.astype(F32)[:, None] * inv_freq[None, :]
    cos, sin = jnp.cos(ang), jnp.sin(ang)
    return _pad_lanes(jnp.concatenate([cos, cos], axis=-1)), _pad_lanes(jnp.concatenate([-sin, sin], axis=-1))


def _sb_scan_operator():
    n = SB_KBLK
    tri = (jnp.arange(n)[:, None] >= jnp.arange(n)[None, :]).astype(F32)
    z = jnp.zeros((n, n), F32)
    return jnp.concatenate([jnp.concatenate([tri, z], axis=1), jnp.concatenate([z, tri], axis=1)], axis=0).astype(BF16)


def _strict_lower(n):
    return (jnp.arange(n)[:, None] > jnp.arange(n)[None, :]).astype(BF16)


def _pad_rows(a, rows):
    return jnp.pad(a, [(0, 0), (0, rows - a.shape[1]), (0, 0)])


def kernel(x_prompt, x_sample, cache_sb_k, cache_sb_v, cache_mla_ckv, cache_mla_kpe, ln1, w_in, q_norm, w_q_up, kv_norm, w_uk,
           w_uv, w_sb_branch, w_mla_branch, w_out, ln2, w_group, b_group, w_router, b_router, w_gate, w_up, w_down, ln_f):
    B, T, D = x_prompt.shape
    SBt, SQ, _ = x_sample.shape
    depth = ln1.shape[0]
    past = cache_sb_k.shape[2]
    sbw = cache_sb_k.shape[3] * cache_sb_k.shape[4]
    sb_hd = cache_sb_k.shape[4]
    qlora = q_norm.shape[1]
    kvlora = kv_norm.shape[1]
    rope = cache_mla_kpe.shape[3]
    nope = w_uk.shape[3]
    vdim = w_uv.shape[3]
    H = w_uk.shape[2]
    sb_scale = 1.0 / math.sqrt(sb_hd)
    mla_scale = 1.0 / math.sqrt(nope + rope)

    cos_p, sin_p = _rope_tables(jnp.arange(T, dtype=jnp.int32), rope)
    pos_s = past + jnp.arange(SQ, dtype=jnp.int32)
    cos_s, sin_s = _rope_tables(jnp.tile(pos_s, SBt), rope)
    tri = _sb_scan_operator()
    eye = jnp.eye(H, dtype=F32)
    row = lambda a: a.reshape(1, -1)

    tm_p = min(512, T)
    tq_mla = min(512, T)
    ns = SBt * SQ
    tk_all = -(-(past + SQ) // SB_KBLK) * SB_KBLK

    xp = x_prompt
    xs = x_sample.reshape(1, ns, D)
    outs_p = [[], [], [], []]
    outs_s = [[], [], [], []]
    for l in range(depth):
        wl = w_in[l]
        c = 3 * sbw + qlora + kvlora
        wkpe = wl[:, c:c + rope]
        wmain = jnp.concatenate([wl[:, :c], _pad_lanes(wkpe), _pad_lanes(_swap_halves(wkpe))], axis=1).astype(BF16)
        wgates = wl[:, c + rope:].astype(BF16)
        wq = w_q_up[l]
        wqn = wq[:, :, :nope].reshape(qlora, H * nope).astype(BF16)
        wqpe = _pad_lanes(wq[:, :, nope:]).reshape(qlora, H * LANES).astype(BF16)
        wqpes = _pad_lanes(_swap_halves(wq[:, :, nope:])).reshape(qlora, H * LANES).astype(BF16)
        wuk_bd = jnp.einsum('hdc,hg->hdgc', jnp.transpose(w_uk[l], (1, 2, 0)), eye).reshape(H * nope, H * kvlora).astype(BF16)
        wuv_bd = jnp.einsum('hcd,hg->hcgd', jnp.transpose(w_uv[l], (1, 0, 2)), eye).reshape(H * kvlora, H * vdim).astype(BF16)
        wr = _pad_lanes(jnp.concatenate([w_group[l], w_router[l]], axis=1))
        wrh = wr.astype(BF16)
        wrl = (wr - wrh.astype(F32)).astype(BF16)
        br = _pad_lanes(jnp.concatenate([b_group[l], b_router[l]]).reshape(1, -1))
        wsb = w_sb_branch[l].astype(BF16)
        wmla = w_mla_branch[l].astype(BF16)
        wo = w_out[l].astype(BF16)
        wg_e = w_gate[l].astype(BF16)
        wu_e = w_up[l].astype(BF16)
        wd_e = w_down[l].astype(BF16)
        lnf = row(ln_f)
        last = l == depth - 1
        proj_args = (row(ln1[l]), wmain, row(q_norm[l]), row(kv_norm[l]), wqn, wuk_bd, wqpe, wqpes)
        proj_kw = dict(sb_scale=sb_scale, mla_scale=mla_scale, sbw=sbw, qlora=qlora, kvlora=kvlora, rope=rope)
        merge_args = (row(ln1[l]), wgates, wuv_bd, wsb, wmla, wo, row(ln2[l]), wrh, wrl, br)

        def moe_block(h, xf, comb, info, cnt):
            n = h.shape[0]
            if n < MOE_GROUPED_MIN_TOKENS:
                return _moe(xf, comb, h, wg_e, wu_e, wd_e, lnf, tm=n, final_norm=last)
            tm_g = min(MOE_GROUP_TILE, n // N_GROUPS)
            tm_tok = min(MOE_ROW_DMA_TILE, n)
            n_tiles = n // tm_g + N_GROUPS
            pos, grp, valid = _group_layout(info, cnt, tm=tm_g, n_tiles=n_tiles)
            xs_rows = _dispatch(pos, xf, n_tiles * tm_g, tm=tm_tok)
            ys_rows = _moe_grouped(grp, valid, xs_rows, wg_e, wu_e, wd_e, wrh, wrl, br, tm=tm_g)
            return _combine(pos, h, lnf, ys_rows, tm=tm_tok, final_norm=last)

        q, k, v, kb, vb, ckv, kpe, _, kcat_t, vext, qcat = _proj(xp, cos_p, sin_p, *proj_args, tm=tm_p, **proj_kw)
        sb_o = _sb_prompt(q, kb, vb, tri)
        lat = _mla_prompt(qcat, kcat_t, vext, tq=tq_mla, tk=tq_mla, kvlora=kvlora)
        y = moe_block(*_merge(xp.reshape(B * T, D), sb_o.reshape(B * T, sbw), lat.reshape(B * T, H * kvlora), *merge_args,
                              _strict_lower(tm_p), tm=tm_p))
        xp = y.reshape(B, T, D)
        for lst, a in zip(outs_p, (k, v, ckv, kpe)):
            lst.append(a)

        q2, k2, v2, kb2, vb2, ckv2, kpe2, kcat2, _, _, qcat2 = _proj(xs, cos_s, sin_s, *proj_args, tm=ns, **proj_kw)
        per_stream = lambda a: a.reshape(SBt, SQ, a.shape[-1])
        k_all = _pad_rows(jnp.concatenate([cache_sb_k[l].reshape(SBt, past, sbw).astype(BF16), per_stream(kb2)], axis=1), tk_all)
        v_all = _pad_rows(jnp.concatenate([cache_sb_v[l].reshape(SBt, past, sbw).astype(BF16), per_stream(vb2)], axis=1), tk_all)
        kc_past = jnp.concatenate([cache_mla_ckv[l], _pad_lanes(cache_mla_kpe[l])], axis=-1).astype(BF16)
        kcat_all = _pad_rows(jnp.concatenate([kc_past, per_stream(kcat2)], axis=1), tk_all)
        sb_o2 = _sb_sample(per_stream(q2), k_all, v_all, tri, qpos0=past)
        qcat_s = jnp.transpose(qcat2.reshape(H, SBt, SQ, kvlora + LANES), (1, 0, 2, 3))
        lat2 = _mla_sample(qcat_s, kcat_all, qpos0=past, nvalid=past + SQ, kvlora=kvlora)
        y2 = moe_block(*_merge(xs.reshape(ns, D), sb_o2.reshape(ns, sbw), lat2.reshape(ns, H * kvlora), *merge_args,
                              _strict_lower(ns), tm=ns))
        xs = y2.reshape(1, ns, D)
        for lst, a in zip(outs_s, (k2, v2, ckv2, kpe2)):
            lst.append(a)

    heads = lambda a, n: a.reshape(n, -1, SB_HEADS, sb_hd)
    return (xp, xs.reshape(SBt, SQ, D),
            jnp.stack([heads(a, B) for a in outs_p[0]]), jnp.stack([heads(a, B) for a in outs_p[1]]),
            jnp.stack(outs_p[2]), jnp.stack(outs_p[3]),
            jnp.stack([heads(a.reshape(SBt, SQ, sbw), SBt) for a in outs_s[0]]),
            jnp.stack([heads(a.reshape(SBt, SQ, sbw), SBt) for a in outs_s[1]]),
            jnp.stack([a.reshape(SBt, SQ, kvlora) for a in outs_s[2]]),
            jnp.stack([a.reshape(SBt, SQ, rope) for a in outs_s[3]]))
```

```python
import functools
import math

import jax
import jax.numpy as jnp
from jax import lax
from jax.experimental import pallas as pl
from jax.experimental.pallas import tpu as pltpu

F32 = jnp.float32
BF16 = jnp.bfloat16

CHUNK = 64
CHUNK_SHIFT = CHUNK.bit_length() - 1
assert 1 << CHUNK_SHIFT == CHUNK
SB_HEADS = 8
MLA_HEADS = 8
ROPE_BASE = 10000.0
N_GROUPS = 4
EXPERTS_PER_GROUP = 8
N_EXPERTS = N_GROUPS * EXPERTS_PER_GROUP
RMS_EPS = 1e-6

LANES = 128
SB_KBLK = LANES
SB_EAGER_BLOCKS = 2
SB_DEAD_CARRY = -110.0
MLA_ROW_GROUPS = 8
MLA_ROW_CHUNK = 64
VMEM_LIMIT = 56 * 1024 * 1024
ROUTER_LANE0 = N_GROUPS
MOE_GROUP_TILE = 512
MOE_ROW_DMA_TILE = 2048
MOE_GROUPED_MIN_TOKENS = 2048


def _dot(a, b):
    return jnp.dot(a, b, preferred_element_type=F32)


def _dot_nt(a, b):
    return lax.dot_general(a, b, (((1,), (1,)), ((), ())), preferred_element_type=F32)


def _rms(x, g):
    return x * lax.rsqrt(jnp.mean(x * x, axis=-1, keepdims=True) + RMS_EPS) * g


def _params(sem):
    return pltpu.CompilerParams(dimension_semantics=sem, vmem_limit_bytes=VMEM_LIMIT)


def _const_spec(shape):
    return pl.BlockSpec(shape, lambda *_: (0,) * len(shape))


def _proj_kernel(x_ref, cos_ref, sin_ref, ln1_ref, wmain_ref, qn_ref, kvn_ref, wqn_ref, wuk_ref, wqpe_ref, wqpes_ref,
                 q_ref, k_ref, v_ref, kb_ref, vb_ref, ckv_ref, kpe_ref, kcat_ref, kcatt_ref, vext_ref, qcat_ref, *, sb_scale, mla_scale, sbw,
                 qlora, kvlora, rope):
    xn = _rms(x_ref[0], ln1_ref[...]).astype(BF16)
    p = _dot(xn, wmain_ref[...])
    o = 0
    q = p[:, o:o + sbw]; o += sbw
    k = p[:, o:o + sbw]; o += sbw
    v = p[:, o:o + sbw]; o += sbw
    cq = p[:, o:o + qlora]; o += qlora
    ckv = p[:, o:o + kvlora]; o += kvlora
    kpe = p[:, o:o + LANES]; o += LANES
    kpe_sw = p[:, o:o + LANES]
    q_ref[0] = (q * sb_scale).astype(BF16)
    k_ref[0] = k
    v_ref[0] = v
    kb_ref[0] = k.astype(BF16)
    vb_ref[0] = v.astype(BF16)
    cos = cos_ref[...]
    sin = sin_ref[...]
    ckvn = _rms(ckv, kvn_ref[...])
    ckv_ref[0] = ckvn
    kpe_rot = kpe * cos + kpe_sw * sin
    kpe_ref[0] = kpe_rot[:, :rope]
    kcat = jnp.concatenate([ckvn, kpe_rot], axis=-1)
    kcat_ref[0] = kcat.astype(BF16)
    kcatt_ref[0] = kcat.T.astype(BF16)
    vext_ref[0] = jnp.concatenate([ckvn, jnp.ones_like(ckvn)], axis=-1).astype(BF16)
    cqn = _rms(cq, qn_ref[...]).astype(BF16)
    qnope = _dot(cqn, wqn_ref[...]).astype(BF16)
    qlat = _dot(qnope, wuk_ref[...])
    qpe = _dot(cqn, wqpe_ref[...])
    qpe_sw = _dot(cqn, wqpes_ref[...])
    for h in range(MLA_HEADS):
        sl = slice(h * LANES, (h + 1) * LANES)
        qcat_ref[0, h, :, :kvlora] = (qlat[:, h * kvlora:(h + 1) * kvlora] * mla_scale).astype(BF16)
        qcat_ref[0, h, :, kvlora:] = ((qpe[:, sl] * cos + qpe_sw[:, sl] * sin) * mla_scale).astype(BF16)


def _proj(x, cos_t, sin_t, ln1, wmain, qn, kvn, wqn, wuk, wqpe, wqpes, *, tm, sb_scale, mla_scale, sbw, qlora, kvlora, rope):
    B, T, D = x.shape
    grid = (B, T // tm)
    tok = lambda w: pl.BlockSpec((1, tm, w), lambda b, i: (b, i, 0))
    tab = pl.BlockSpec((tm, LANES), lambda b, i: (i, 0))
    kcw = kvlora + LANES
    out_shape = (
        jax.ShapeDtypeStruct((B, T, sbw), BF16),
        jax.ShapeDtypeStruct((B, T, sbw), F32),
        jax.ShapeDtypeStruct((B, T, sbw), F32),
        jax.ShapeDtypeStruct((B, T, sbw), BF16),
        jax.ShapeDtypeStruct((B, T, sbw), BF16),
        jax.ShapeDtypeStruct((B, T, kvlora), F32),
        jax.ShapeDtypeStruct((B, T, rope), F32),
        jax.ShapeDtypeStruct((B, T, kcw), BF16),
        jax.ShapeDtypeStruct((B, kcw, T), BF16),
        jax.ShapeDtypeStruct((B, T, 2 * kvlora), BF16),
        jax.ShapeDtypeStruct((B, MLA_HEADS, T, kcw), BF16),
    )
    out_specs = (tok(sbw), tok(sbw), tok(sbw), tok(sbw), tok(sbw), tok(kvlora), tok(rope), tok(kcw),
                 pl.BlockSpec((1, kcw, tm), lambda b, i: (b, 0, i)), tok(2 * kvlora),
                 pl.BlockSpec((1, MLA_HEADS, tm, kcw), lambda b, i: (b, 0, i, 0)))
    in_specs = [tok(D), tab, tab] + [_const_spec(a.shape) for a in (ln1, wmain, qn, kvn, wqn, wuk, wqpe, wqpes)]
    kern = functools.partial(_proj_kernel, sb_scale=sb_scale, mla_scale=mla_scale, sbw=sbw, qlora=qlora, kvlora=kvlora,
                             rope=rope)
    return pl.pallas_call(kern, out_shape=out_shape, grid=grid, in_specs=in_specs, out_specs=out_specs,
                          compiler_params=_params(("parallel", "parallel")), name="proj")(
        x, cos_t, sin_t, ln1, wmain, qn, kvn, wqn, wuk, wqpe, wqpes)


def _head_pair_blockdiag(blk):
    lane = lax.broadcasted_iota(jnp.int32, blk.shape, 1)
    zero = jnp.zeros_like(blk)
    half = LANES // 2
    return jnp.concatenate([jnp.where(lane < half, blk, zero), jnp.where(lane >= half, blk, zero)], axis=0)


def _sb_step(qs, kblk, vblk, tri, carries, accs, keep):
    tq = qs[0].shape[0]
    lbs, loms, parts = [], [], []
    for p, q in enumerate(qs):
        z = _dot_nt(q, _head_pair_blockdiag(kblk[:, p * LANES:(p + 1) * LANES]))
        lb = jnp.minimum(z, 0.0) - jnp.log(1.0 + jnp.exp(-jnp.abs(z)))
        lom = lb - z
        if keep is not None:
            lom = jnp.where(keep, lom, 0.0)
        hi = lom.astype(BF16)
        lbs.append(lb)
        loms.append(lom)
        parts += [hi, (lom - hi.astype(F32)).astype(BF16)]
    r = _dot(jnp.concatenate(parts, axis=0), tri)
    new_c, new_a = [], []
    for p in range(len(qs)):
        incl = r[2 * p * tq:(2 * p + 1) * tq] + r[(2 * p + 1) * tq:(2 * p + 2) * tq]
        w = jnp.exp(lbs[p] + (incl - loms[p]) + carries[p])
        if keep is not None:
            w = jnp.where(keep, w, 0.0)
        new_a.append(accs[p] + _dot(w.astype(BF16), _head_pair_blockdiag(vblk[:, p * LANES:(p + 1) * LANES])))
        tot = jnp.concatenate([jnp.broadcast_to(incl[:, 0:1], (tq, SB_KBLK)),
                               jnp.broadcast_to(incl[:, SB_KBLK:SB_KBLK + 1], (tq, SB_KBLK))], axis=1)
        new_c.append(carries[p] + tot)
    return new_c, new_a


def _sb_prompt_kernel(q_ref, k_ref, v_ref, tri_ref, o_ref, carry_ref, acc_ref, *, tq):
    i = pl.program_id(1)
    npair = q_ref.shape[2] // LANES
    qs = [q_ref[0, :, p * LANES:(p + 1) * LANES] for p in range(npair)]
    tri = tri_ref[...]
    row = lax.broadcasted_iota(jnp.int32, (tq, 2 * SB_KBLK), 0)
    col = lax.broadcasted_iota(jnp.int32, (tq, 2 * SB_KBLK), 1)
    ks = pl.multiple_of(i * SB_KBLK, SB_KBLK)
    zeros_c = [jnp.zeros((tq, 2 * SB_KBLK), F32)] * npair
    zeros_a = [jnp.zeros((tq, LANES), F32)] * npair
    c, a = _sb_step(qs, k_ref[0, pl.ds(ks, SB_KBLK), :], v_ref[0, pl.ds(ks, SB_KBLK), :], tri, zeros_c, zeros_a,
                    (col & (SB_KBLK - 1)) < row)
    for d in range(1, SB_EAGER_BLOCKS + 1):
        kp = pl.multiple_of(jnp.maximum(i - d, 0) * SB_KBLK, SB_KBLK)
        vprev = v_ref[0, pl.ds(kp, SB_KBLK), :]
        c, a = _sb_step(qs, k_ref[0, pl.ds(kp, SB_KBLK), :], jnp.where(i >= d, vprev, jnp.zeros_like(vprev)), tri, c, a,
                        None)
    for p in range(npair):
        carry_ref[p] = c[p]
        acc_ref[p] = a[p]

    def cond(s):
        j, live = s
        return jnp.logical_and(j >= 0, live > SB_DEAD_CARRY)

    def body(s):
        j, _ = s
        ks = pl.multiple_of(j * SB_KBLK, SB_KBLK)
        c, a = _sb_step(qs, k_ref[0, pl.ds(ks, SB_KBLK), :], v_ref[0, pl.ds(ks, SB_KBLK), :], tri,
                        [carry_ref[p] for p in range(npair)], [acc_ref[p] for p in range(npair)], None)
        for p in range(npair):
            carry_ref[p] = c[p]
            acc_ref[p] = a[p]
        return j - 1, jnp.max(functools.reduce(jnp.maximum, c))

    lax.while_loop(cond, body, (i - 1 - SB_EAGER_BLOCKS, jnp.max(functools.reduce(jnp.maximum, c))))
    for p in range(npair):
        o_ref[0, :, p * LANES:(p + 1) * LANES] = acc_ref[p].astype(o_ref.dtype)


def _sb_prompt(q, kb, vb, tri):
    B, T, W = q.shape
    tq = SB_KBLK
    npair = W // LANES
    qspec = pl.BlockSpec((1, tq, W), lambda b, i: (b, i, 0))
    kspec = pl.BlockSpec((1, T, W), lambda b, i: (b, 0, 0))
    return pl.pallas_call(
        functools.partial(_sb_prompt_kernel, tq=tq), out_shape=jax.ShapeDtypeStruct((B, T, W), BF16), grid=(B, T // tq),
        in_specs=[qspec, kspec, kspec, _const_spec(tri.shape)], out_specs=qspec,
        scratch_shapes=[pltpu.VMEM((npair, tq, 2 * SB_KBLK), F32), pltpu.VMEM((npair, tq, LANES), F32)],
        compiler_params=_params(("parallel", "arbitrary")), name="sb_prompt")(q, kb, vb, tri)


def _sb_sample_kernel(q_ref, k_ref, v_ref, tri_ref, o_ref, *, tq, nkb, qpos0):
    nstream = q_ref.shape[0]
    npair = q_ref.shape[2] // LANES
    tri = tri_ref[...]
    row = lax.broadcasted_iota(jnp.int32, (tq, 2 * SB_KBLK), 0)
    col = lax.broadcasted_iota(jnp.int32, (tq, 2 * SB_KBLK), 1)
    qs = [[q_ref[s, :, p * LANES:(p + 1) * LANES] for p in range(npair)] for s in range(nstream)]
    c = [[jnp.zeros((tq, 2 * SB_KBLK), F32)] * npair for _ in range(nstream)]
    a = [[jnp.zeros((tq, LANES), F32)] * npair for _ in range(nstream)]
    for j in range(nkb - 1, -1, -1):
        ks = j * SB_KBLK
        keep = None if ks + SB_KBLK <= qpos0 else (ks + (col & (SB_KBLK - 1))) < (qpos0 + row)
        for s in range(nstream):
            c[s], a[s] = _sb_step(qs[s], k_ref[s, ks:ks + SB_KBLK, :], v_ref[s, ks:ks + SB_KBLK, :], tri, c[s], a[s], keep)
    for s in range(nstream):
        for p in range(npair):
            o_ref[s, :, p * LANES:(p + 1) * LANES] = a[s][p].astype(o_ref.dtype)


def _sb_sample(q, k_all, v_all, tri, *, qpos0):
    B, tq, W = q.shape
    Tk = k_all.shape[1]
    nstream = 2 if B % 2 == 0 else 1
    qspec = pl.BlockSpec((nstream, tq, W), lambda b: (b, 0, 0))
    kspec = pl.BlockSpec((nstream, Tk, W), lambda b: (b, 0, 0))
    return pl.pallas_call(
        functools.partial(_sb_sample_kernel, tq=tq, nkb=Tk // SB_KBLK, qpos0=qpos0),
        out_shape=jax.ShapeDtypeStruct((B, tq, W), BF16), grid=(B // nstream,),
        in_specs=[qspec, kspec, kspec, _const_spec(tri.shape)], out_specs=qspec,
        compiler_params=_params(("parallel",)), name="sb_sample")(q, k_all, v_all, tri)


def _mla_prompt_kernel(q_ref, kt_ref, vx_ref, o_ref, s_ref, p_ref, m_ref, a_ref, acc_ref, *, tq, tk, kvlora):
    i = pl.program_id(1)
    H = q_ref.shape[1]
    kc = q_ref.shape[3]
    group_heads = H // MLA_ROW_GROUPS
    gr = group_heads * tq
    m_ref[...] = jnp.full(m_ref.shape, -jnp.inf, F32)
    acc_ref[...] = jnp.zeros(acc_ref.shape, F32)

    def step(j, masked):
        ks = pl.multiple_of(j * tk, tk)
        kt = kt_ref[0, :, pl.ds(ks, tk)]
        vx = vx_ref[0, pl.ds(ks, tk), :]
        for g in range(MLA_ROW_GROUPS):
            qg = q_ref[0, g * group_heads:(g + 1) * group_heads].reshape(gr, kc)
            s_ref[g * gr:(g + 1) * gr, :] = _dot(qg, kt)
        def score_tiles(r0):
            rs = slice(r0, r0 + MLA_ROW_CHUNK)
            ntile = tk // LANES
            if masked:
                ntile = ((r0 % tq) + MLA_ROW_CHUNK - 1) // LANES + 1
            tiles = [s_ref[rs, t * LANES:(t + 1) * LANES] for t in range(ntile)]
            if masked:
                row = lax.broadcasted_iota(jnp.int32, (MLA_ROW_CHUNK, LANES), 0)
                col = lax.broadcasted_iota(jnp.int32, (MLA_ROW_CHUNK, LANES), 1)
                qchunk = (i * tq + (r0 % tq) + row) >> CHUNK_SHIFT
                tiles = [jnp.where(((j * tk + t * LANES + col) >> CHUNK_SHIFT) <= qchunk, st, -jnp.inf)
                         for t, st in enumerate(tiles)]
            return tiles

        for g in range(MLA_ROW_GROUPS):
            for c in range(gr // MLA_ROW_CHUNK):
                r0 = g * gr + c * MLA_ROW_CHUNK
                rs = slice(r0, r0 + MLA_ROW_CHUNK)
                tiles = score_tiles(r0)
                m_old = m_ref[rs, :]
                m_new = jnp.maximum(m_old, jnp.max(functools.reduce(jnp.maximum, tiles), axis=-1, keepdims=True))
                for t in range(len(tiles), tk // LANES):
                    p_ref[rs, t * LANES:(t + 1) * LANES] = jnp.zeros((MLA_ROW_CHUNK, LANES), BF16)
                for t, st in enumerate(tiles):
                    p_ref[rs, t * LANES:(t + 1) * LANES] = jnp.exp(st - m_new).astype(BF16)
                a_ref[rs, :] = jnp.exp(m_old - m_new)
                m_ref[rs, :] = m_new
            gs = slice(g * gr, (g + 1) * gr)
            alpha = a_ref[gs, :]
            acc_ref[gs, :] = jnp.concatenate([alpha, alpha], axis=-1) * acc_ref[gs, :] + _dot(p_ref[gs, :], vx)

    nfull = (i * tq) // tk

    def body(j, c):
        step(j, False)
        return c

    lax.fori_loop(0, nfull, body, 0)
    for d in range(tq // tk):
        step(nfull + d, True)
    for h in range(H):
        hs = slice(h * tq, (h + 1) * tq)
        o_ref[0, :, h * kvlora:(h + 1) * kvlora] = (acc_ref[hs, :kvlora] / acc_ref[hs, kvlora:]).astype(o_ref.dtype)


def _mla_prompt(qcat, kcat_t, vext, *, tq, tk, kvlora):
    B, H, T, KC = qcat.shape
    grid = (B, T // tq)
    rows = H * tq
    return pl.pallas_call(
        functools.partial(_mla_prompt_kernel, tq=tq, tk=tk, kvlora=kvlora),
        out_shape=jax.ShapeDtypeStruct((B, T, H * kvlora), BF16), grid=grid,
        in_specs=[pl.BlockSpec((1, H, tq, KC), lambda b, i: (b, 0, i, 0)), pl.BlockSpec((1, KC, T), lambda b, i: (b, 0, 0)),
                  pl.BlockSpec((1, T, 2 * kvlora), lambda b, i: (b, 0, 0))],
        out_specs=pl.BlockSpec((1, tq, H * kvlora), lambda b, i: (b, i, 0)),
        scratch_shapes=[pltpu.VMEM((rows, tk), F32), pltpu.VMEM((rows, tk), BF16), pltpu.VMEM((rows, LANES), F32),
                        pltpu.VMEM((rows, LANES), F32), pltpu.VMEM((rows, 2 * kvlora), F32)],
        compiler_params=_params(("parallel", "arbitrary")), name="mla_prompt")(qcat, kcat_t, vext)


def _mla_sample_kernel(q_ref, kc_ref, o_ref, *, tq, qpos0, nvalid, kvlora):
    H = q_ref.shape[1]
    q = q_ref[0].reshape(H * tq, q_ref.shape[3])
    kall = kc_ref[0]
    s = _dot_nt(q, kall)
    row = lax.broadcasted_iota(jnp.int32, s.shape, 0)
    col = lax.broadcasted_iota(jnp.int32, s.shape, 1)
    qpos = qpos0 + (row & (tq - 1))
    keep = jnp.logical_and((col >> CHUNK_SHIFT) <= (qpos >> CHUNK_SHIFT), col < nvalid)
    s = jnp.where(keep, s, -jnp.inf)
    p = jnp.exp(s - jnp.max(s, axis=-1, keepdims=True))
    out = _dot(p.astype(BF16), kall[:, :kvlora]) / jnp.sum(p, axis=-1, keepdims=True)
    for h in range(H):
        o_ref[0, :, h * kvlora:(h + 1) * kvlora] = out[h * tq:(h + 1) * tq].astype(o_ref.dtype)


def _mla_sample(qcat, kcat_all, *, qpos0, nvalid, kvlora):
    B, H, tq, KC = qcat.shape
    Tk = kcat_all.shape[1]
    return pl.pallas_call(
        functools.partial(_mla_sample_kernel, tq=tq, qpos0=qpos0, nvalid=nvalid, kvlora=kvlora),
        out_shape=jax.ShapeDtypeStruct((B, tq, H * kvlora), BF16), grid=(B,),
        in_specs=[pl.BlockSpec((1, H, tq, KC), lambda b: (b, 0, 0, 0)), pl.BlockSpec((1, Tk, KC), lambda b: (b, 0, 0))],
        out_specs=pl.BlockSpec((1, tq, H * kvlora), lambda b: (b, 0, 0)),
        compiler_params=_params(("parallel",)), name="mla_sample")(qcat, kcat_all)


def _router_logits(xf_bf, wrh_ref, wrl_ref, br_ref):
    return _dot(xf_bf, wrh_ref[...]) + _dot(xf_bf, wrl_ref[...]) + br_ref[...]


def _route(logits, gidx=None):
    lane = lax.broadcasted_iota(jnp.int32, logits.shape, 1)
    ninf = -jnp.inf
    gl = jnp.where(lane < N_GROUPS, logits, ninf)
    gmax = jnp.max(gl, axis=-1, keepdims=True)
    gexp = jnp.where(lane < N_GROUPS, jnp.exp(logits - gmax), 0.0)
    if gidx is None:
        gidx = jnp.min(jnp.where(gl == gmax, lane, LANES), axis=-1, keepdims=True)
    g_val = jnp.sum(jnp.where(lane == gidx, gexp, 0.0), axis=-1, keepdims=True) / jnp.sum(gexp, axis=-1, keepdims=True)
    lo = ROUTER_LANE0 + EXPERTS_PER_GROUP * gidx
    el = jnp.where(jnp.logical_and(lane >= lo, lane < lo + EXPERTS_PER_GROUP), logits, ninf)
    v1 = jnp.max(el, axis=-1, keepdims=True)
    i1 = jnp.min(jnp.where(el == v1, lane, LANES), axis=-1, keepdims=True)
    el2 = jnp.where(lane == i1, ninf, el)
    v2 = jnp.max(el2, axis=-1, keepdims=True)
    i2 = jnp.min(jnp.where(el2 == v2, lane, LANES), axis=-1, keepdims=True)
    e2 = jnp.exp(v2 - v1)
    den = 1.0 + e2
    return jnp.where(lane == i1, (1.0 / den) * g_val, 0.0) + jnp.where(lane == i2, (e2 / den) * g_val, 0.0), gidx


def _merge_kernel(x_ref, sb_ref, lat_ref, ln1_ref, wg_ref, wuv_ref, wsb_ref, wmla_ref, wout_ref, ln2_ref, wrh_ref, wrl_ref,
                  br_ref, tril_ref, h_ref, xf_ref, comb_ref, info_ref, cnt_ref, *, d):
    @pl.when(pl.program_id(0) == 0)
    def _():
        cnt_ref[...] = jnp.zeros(cnt_ref.shape, F32)

    x = x_ref[...]
    xn = _rms(x, ln1_ref[...]).astype(BF16)
    g = _dot(xn, wg_ref[...])
    mla_o = _dot(lat_ref[...], wuv_ref[...]).astype(BF16)
    merged = jax.nn.sigmoid(g[:, :d]) * _dot(sb_ref[...], wsb_ref[...]) + jax.nn.sigmoid(g[:, d:]) * _dot(mla_o, wmla_ref[...])
    h = x + _dot(merged.astype(BF16), wout_ref[...])
    h_ref[...] = h
    xf = _rms(h, ln2_ref[...]).astype(BF16)
    xf_ref[...] = xf
    comb, gidx = _route(_router_logits(xf, wrh_ref, wrl_ref, br_ref))
    comb_ref[...] = comb
    lane = lax.broadcasted_iota(jnp.int32, comb.shape, 1)
    onehot = lane == gidx
    before = _dot(tril_ref[...], jnp.where(onehot, 1.0, 0.0).astype(BF16)) + cnt_ref[...]
    rank = jnp.sum(jnp.where(onehot, before, 0.0), axis=-1, keepdims=True).astype(jnp.int32)
    info_ref[...] = jnp.where(lane == 0, rank, jnp.where(lane == 1, gidx, 0))
    cnt_ref[...] += jnp.sum(jnp.where(onehot, 1.0, 0.0), axis=0, keepdims=True)


def _merge(x, sb_o, lat, ln1, wg, wuv, wsb, wmla, wout, ln2, wrh, wrl, br, tril, *, tm):
    N, D = x.shape
    tok = lambda w: pl.BlockSpec((tm, w), lambda i: (i, 0))
    consts = (ln1, wg, wuv, wsb, wmla, wout, ln2, wrh, wrl, br, tril)
    return pl.pallas_call(
        functools.partial(_merge_kernel, d=D),
        out_shape=(jax.ShapeDtypeStruct((N, D), F32), jax.ShapeDtypeStruct((N, D), BF16), jax.ShapeDtypeStruct((N, LANES), F32),
                   jax.ShapeDtypeStruct((N, LANES), jnp.int32), jax.ShapeDtypeStruct((1, LANES), F32)),
        grid=(N // tm,), in_specs=[tok(D), tok(sb_o.shape[1]), tok(lat.shape[1])] + [_const_spec(a.shape) for a in consts],
        out_specs=(tok(D), tok(D), tok(LANES), tok(LANES), _const_spec((1, LANES))),
        compiler_params=_params(("arbitrary",)), name="merge")(x, sb_o, lat, *consts)


def _moe_kernel(xf_ref, comb_ref, h_ref, wg_ref, wu_ref, wd_ref, lnf_ref, y_ref, acc_ref, *, final_norm):
    e = pl.program_id(1)

    @pl.when(e == 0)
    def _():
        acc_ref[...] = jnp.zeros(acc_ref.shape, F32)

    xf = xf_ref[...]
    comb = comb_ref[...]
    lane = lax.broadcasted_iota(jnp.int32, comb.shape, 1)
    c = jnp.sum(jnp.where(lane == e + ROUTER_LANE0, comb, 0.0), axis=-1, keepdims=True)
    act = jax.nn.silu(_dot(xf, wg_ref[0])) * _dot(xf, wu_ref[0])
    acc_ref[...] += _dot((act * c).astype(BF16), wd_ref[0])

    @pl.when(e == pl.num_programs(1) - 1)
    def _():
        y = h_ref[...] + acc_ref[...]
        y_ref[...] = _rms(y, lnf_ref[...]) if final_norm else y


def _moe(xf, comb, h, wg, wu, wd, lnf, *, tm, final_norm):
    N, D = h.shape
    E, _, F = wg.shape
    tok = lambda w: pl.BlockSpec((tm, w), lambda i, e: (i, 0))
    return pl.pallas_call(
        functools.partial(_moe_kernel, final_norm=final_norm), out_shape=jax.ShapeDtypeStruct((N, D), F32), grid=(N // tm, E),
        in_specs=[tok(D), tok(LANES), tok(D), pl.BlockSpec((1, D, F), lambda i, e: (e, 0, 0)),
                  pl.BlockSpec((1, D, F), lambda i, e: (e, 0, 0)), pl.BlockSpec((1, F, D), lambda i, e: (e, 0, 0)),
                  pl.BlockSpec((1, D), lambda i, e: (0, 0))],
        out_specs=tok(D), scratch_shapes=[pltpu.VMEM((tm, D), F32)],
        compiler_params=_params(("parallel", "arbitrary")), name="moe")(xf, comb, h, wg, wu, wd, lnf)


def _to_row_tiles(ref, x):
    tiles = x.shape[1] // LANES
    for k in range(tiles):
        ref[pl.ds(k, x.shape[0], stride=tiles), :] = x[:, k * LANES:(k + 1) * LANES]


def _from_row_tiles(ref, tiles):
    n = ref.shape[0] // tiles
    return jnp.concatenate([ref[pl.ds(k, n, stride=tiles), :] for k in range(tiles)], axis=-1)


def _row_tile(ref, r, tiles):
    return ref.at[pl.ds(pl.multiple_of(r * tiles, tiles), tiles)]


ROW_DMA_UNROLL = 8


def _row_dma_issue(n, make_copy):
    def issue(k, c):
        for u in range(ROW_DMA_UNROLL):
            make_copy(k * ROW_DMA_UNROLL + u).start(priority=u % 2)
        return c

    lax.fori_loop(0, n // ROW_DMA_UNROLL, issue, 0)


def _row_dma_drain(n, make_copy):
    def drain(r, c):
        make_copy(0).wait()
        return c

    lax.fori_loop(0, n, drain, 0, unroll=ROW_DMA_UNROLL)


def _row_dma_loop(n, make_copy):
    _row_dma_issue(n, make_copy)
    _row_dma_drain(n, make_copy)


def _dispatch_kernel(pos_hbm, xf_ref, init_hbm, out_hbm, idx_ref, buf_ref, sem_idx, sem_rows, *, tm):
    del init_hbm
    idx_copy = pltpu.make_async_copy(pos_hbm.at[pl.program_id(0)], idx_ref, sem_idx)
    idx_copy.start()
    _to_row_tiles(buf_ref, xf_ref[...].astype(F32))
    idx_copy.wait()
    tiles = out_hbm.shape[1]
    _row_dma_loop(tm, lambda r: pltpu.make_async_copy(_row_tile(buf_ref, r, tiles), out_hbm.at[idx_ref[0, r]], sem_rows))


def _dispatch(pos, xf, n_rows, *, tm):
    N, D = xf.shape
    tiles = D // LANES
    init = jnp.zeros((n_rows, tiles, LANES), F32)
    anyspec = pl.BlockSpec(memory_space=pl.ANY)
    return pl.pallas_call(
        functools.partial(_dispatch_kernel, tm=tm), out_shape=jax.ShapeDtypeStruct(init.shape, F32), grid=(N // tm,),
        in_specs=[anyspec, pl.BlockSpec((tm, D), lambda i: (i, 0)), anyspec], out_specs=anyspec,
        scratch_shapes=[pltpu.SMEM((1, tm), jnp.int32), pltpu.VMEM((tm * tiles, LANES), F32), pltpu.SemaphoreType.DMA,
                        pltpu.SemaphoreType.DMA],
        input_output_aliases={2: 0}, compiler_params=_params(("arbitrary",)), name="moe_dispatch")(
        pos.reshape(N // tm, 1, tm), xf, init)


def _moe_grouped_kernel(grp_ref, valid_ref, xs_ref, wg_ref, wu_ref, wd_ref, wrh_ref, wrl_ref, br_ref, ys_ref, *, tiles):
    w = pl.program_id(0)
    grp = grp_ref[w]

    @pl.when(valid_ref[w] == 0)
    def _():
        ys_ref[...] = jnp.zeros(ys_ref.shape, F32)

    @pl.when(valid_ref[w] == 1)
    def _():
        x = _from_row_tiles(xs_ref, tiles).astype(BF16)
        comb = _route(_router_logits(x, wrh_ref, wrl_ref, br_ref), grp)[0]
        lane = lax.broadcasted_iota(jnp.int32, comb.shape, 1)
        lane0 = ROUTER_LANE0 + EXPERTS_PER_GROUP * grp
        out = None
        for e in range(EXPERTS_PER_GROUP):
            c = jnp.sum(jnp.where(lane == lane0 + e, comb, 0.0), axis=-1, keepdims=True)
            act = jax.nn.silu(_dot(x, wg_ref[e])) * _dot(x, wu_ref[e])
            y = _dot((act * c).astype(BF16), wd_ref[e])
            out = y if out is None else out + y
        _to_row_tiles(ys_ref, out)


def _moe_grouped(grp, valid, xs, wg, wu, wd, wrh, wrl, br, *, tm):
    S, tiles, _ = xs.shape
    D = tiles * LANES
    F = wg.shape[2]
    rows = pl.BlockSpec((tm * tiles, LANES), lambda w, grp, valid: (w, 0))
    group = lambda w, grp, valid: (grp[w], 0, 0)
    const = lambda shape: pl.BlockSpec(shape, lambda w, grp, valid: (0,) * len(shape))
    grid_spec = pltpu.PrefetchScalarGridSpec(
        num_scalar_prefetch=2, grid=(S // tm,),
        in_specs=[rows, pl.BlockSpec((EXPERTS_PER_GROUP, D, F), group), pl.BlockSpec((EXPERTS_PER_GROUP, D, F), group),
                  pl.BlockSpec((EXPERTS_PER_GROUP, F, D), group), const(wrh.shape), const(wrl.shape), const(br.shape)],
        out_specs=rows)
    flat = (S * tiles, LANES)
    return pl.pallas_call(functools.partial(_moe_grouped_kernel, tiles=tiles), out_shape=jax.ShapeDtypeStruct(flat, F32),
                          grid_spec=grid_spec, compiler_params=_params(("arbitrary",)), name="moe_grouped")(
        grp, valid, xs.reshape(flat), wg, wu, wd, wrh, wrl, br).reshape(xs.shape)


def _combine_kernel(pos_hbm, h_ref, lnf_ref, ys_hbm, y_ref, idx_ref, buf_ref, sem_idx, sem_rows, *, tm, final_norm):
    i = pl.program_id(0)
    slot = i % 2
    tiles = ys_hbm.shape[1]

    def rows(s):
        return lambda r: pltpu.make_async_copy(ys_hbm.at[idx_ref[0, r]], _row_tile(buf_ref.at[s], r, tiles), sem_rows.at[s])

    def start_gather(step, s):
        idx_copy = pltpu.make_async_copy(pos_hbm.at[step], idx_ref, sem_idx)
        idx_copy.start()
        idx_copy.wait()
        _row_dma_issue(tm, rows(s))

    @pl.when(i == 0)
    def _():
        start_gather(0, 0)

    @pl.when(i + 1 < pl.num_programs(0))
    def _():
        start_gather(i + 1, 1 - slot)

    _row_dma_drain(tm, rows(slot))
    y = h_ref[...] + _from_row_tiles(buf_ref.at[slot], tiles)
    y_ref[...] = _rms(y, lnf_ref[...]) if final_norm else y


def _combine(pos, h, lnf, ys, *, tm, final_norm):
    N, D = h.shape
    tiles = D // LANES
    anyspec = pl.BlockSpec(memory_space=pl.ANY)
    return pl.pallas_call(
        functools.partial(_combine_kernel, tm=tm, final_norm=final_norm), out_shape=jax.ShapeDtypeStruct((N, D), F32),
        grid=(N // tm,), in_specs=[anyspec, pl.BlockSpec((tm, D), lambda i: (i, 0)), _const_spec(lnf.shape), anyspec],
        out_specs=pl.BlockSpec((tm, D), lambda i: (i, 0)),
        scratch_shapes=[pltpu.SMEM((1, tm), jnp.int32), pltpu.VMEM((2, tm * tiles, LANES), F32), pltpu.SemaphoreType.DMA,
                        pltpu.SemaphoreType.DMA((2,))],
        compiler_params=_params(("arbitrary",)), name="moe_combine")(pos.reshape(N // tm, 1, tm), h, lnf, ys)


def _group_layout(info, cnt, *, tm, n_tiles):
    rank, gidx = info[:, 0], info[:, 1]
    counts = cnt[0, :N_GROUPS].astype(jnp.int32)
    ntiles = (counts + tm - 1) // tm
    tile_end = jnp.cumsum(ntiles)
    pos = (tile_end - ntiles)[gidx] * tm + rank
    tile = jnp.arange(n_tiles, dtype=jnp.int32)
    valid = (tile < tile_end[-1]).astype(jnp.int32)
    grp = jnp.sum((jnp.minimum(tile, tile_end[-1] - 1)[:, None] >= tile_end[None, :]).astype(jnp.int32), axis=1)
    return pos, grp, valid


def _pad_lanes(w):
    return jnp.pad(w, [(0, 0)] * (w.ndim - 1) + [(0, LANES - w.shape[-1])])


def _swap_halves(w):
    half = w.shape[-1] // 2
    return jnp.concatenate([w[..., half:], w[..., :half]], axis=-1)


def _rope_tables(pos, rope):
    half = rope // 2
    inv_freq = ROPE_BASE ** (-jnp.arange(half, dtype=F32) / half)
    ang = pos.astype(F32)[:, None] * inv_freq[None, :]
    cos, sin = jnp.cos(ang), jnp.sin(ang)
    return _pad_lanes(jnp.concatenate([cos, cos], axis=-1)), _pad_lanes(jnp.concatenate([-sin, sin], axis=-1))


def _sb_scan_operator():
    n = SB_KBLK
    tri = (jnp.arange(n)[:, None] >= jnp.arange(n)[None, :]).astype(F32)
    z = jnp.zeros((n, n), F32)
    return jnp.concatenate([jnp.concatenate([tri, z], axis=1), jnp.concatenate([z, tri], axis=1)], axis=0).astype(BF16)


def _strict_lower(n):
    return (jnp.arange(n)[:, None] > jnp.arange(n)[None, :]).astype(BF16)


def _pad_rows(a, rows):
    return jnp.pad(a, [(0, 0), (0, rows - a.shape[1]), (0, 0)])


def kernel(x_prompt, x_sample, cache_sb_k, cache_sb_v, cache_mla_ckv, cache_mla_kpe, ln1, w_in, q_norm, w_q_up, kv_norm, w_uk,
           w_uv, w_sb_branch, w_mla_branch, w_out, ln2, w_group, b_group, w_router, b_router, w_gate, w_up, w_down, ln_f):
    B, T, D = x_prompt.shape
    SBt, SQ, _ = x_sample.shape
    depth = ln1.shape[0]
    past = cache_sb_k.shape[2]
    sbw = cache_sb_k.shape[3] * cache_sb_k.shape[4]
    sb_hd = cache_sb_k.shape[4]
    qlora = q_norm.shape[1]
    kvlora = kv_norm.shape[1]
    rope = cache_mla_kpe.shape[3]
    nope = w_uk.shape[3]
    vdim = w_uv.shape[3]
    H = w_uk.shape[2]
    sb_scale = 1.0 / math.sqrt(sb_hd)
    mla_scale = 1.0 / math.sqrt(nope + rope)

    cos_p, sin_p = _rope_tables(jnp.arange(T, dtype=jnp.int32), rope)
    pos_s = past + jnp.arange(SQ, dtype=jnp.int32)
    cos_s, sin_s = _rope_tables(jnp.tile(pos_s, SBt), rope)
    tri = _sb_scan_operator()
    eye = jnp.eye(H, dtype=F32)
    row = lambda a: a.reshape(1, -1)

    tm_p = min(512, T)
    tq_mla = min(512, T)
    ns = SBt * SQ
    tk_all = -(-(past + SQ) // SB_KBLK) * SB_KBLK

    xp = x_prompt
    xs = x_sample.reshape(1, ns, D)
    outs_p = [[], [], [], []]
    outs_s = [[], [], [], []]
    for l in range(depth):
        wl = w_in[l]
        c = 3 * sbw + qlora + kvlora
        wkpe = wl[:, c:c + rope]
        wmain = jnp.concatenate([wl[:, :c], _pad_lanes(wkpe), _pad_lanes(_swap_halves(wkpe))], axis=1).astype(BF16)
        wgates = wl[:, c + rope:].astype(BF16)
        wq = w_q_up[l]
        wqn = wq[:, :, :nope].reshape(qlora, H * nope).astype(BF16)
        wqpe = _pad_lanes(wq[:, :, nope:]).reshape(qlora, H * LANES).astype(BF16)
        wqpes = _pad_lanes(_swap_halves(wq[:, :, nope:])).reshape(qlora, H * LANES).astype(BF16)
        wuk_bd = jnp.einsum('hdc,hg->hdgc', jnp.transpose(w_uk[l], (1, 2, 0)), eye).reshape(H * nope, H * kvlora).astype(BF16)
        wuv_bd = jnp.einsum('hcd,hg->hcgd', jnp.transpose(w_uv[l], (1, 0, 2)), eye).reshape(H * kvlora, H * vdim).astype(BF16)
        wr = _pad_lanes(jnp.concatenate([w_group[l], w_router[l]], axis=1))
        wrh = wr.astype(BF16)
        wrl = (wr - wrh.astype(F32)).astype(BF16)
        br = _pad_lanes(jnp.concatenate([b_group[l], b_router[l]]).reshape(1, -1))
        wsb = w_sb_branch[l].astype(BF16)
        wmla = w_mla_branch[l].astype(BF16)
        wo = w_out[l].astype(BF16)
        wg_e = w_gate[l].astype(BF16)
        wu_e = w_up[l].astype(BF16)
        wd_e = w_down[l].astype(BF16)
        lnf = row(ln_f)
        last = l == depth - 1
        proj_args = (row(ln1[l]), wmain, row(q_norm[l]), row(kv_norm[l]), wqn, wuk_bd, wqpe, wqpes)
        proj_kw = dict(sb_scale=sb_scale, mla_scale=mla_scale, sbw=sbw, qlora=qlora, kvlora=kvlora, rope=rope)
        merge_args = (row(ln1[l]), wgates, wuv_bd, wsb, wmla, wo, row(ln2[l]), wrh, wrl, br)

        def moe_block(h, xf, comb, info, cnt):
            n = h.shape[0]
            if n < MOE_GROUPED_MIN_TOKENS:
                return _moe(xf, comb, h, wg_e, wu_e, wd_e, lnf, tm=n, final_norm=last)
            tm_g = min(MOE_GROUP_TILE, n // N_GROUPS)
            tm_tok = min(MOE_ROW_DMA_TILE, n)
            n_tiles = n // tm_g + N_GROUPS
            pos, grp, valid = _group_layout(info, cnt, tm=tm_g, n_tiles=n_tiles)
            xs_rows = _dispatch(pos, xf, n_tiles * tm_g, tm=tm_tok)
            ys_rows = _moe_grouped(grp, valid, xs_rows, wg_e, wu_e, wd_e, wrh, wrl, br, tm=tm_g)
            return _combine(pos, h, lnf, ys_rows, tm=tm_tok, final_norm=last)

        q, k, v, kb, vb, ckv, kpe, _, kcat_t, vext, qcat = _proj(xp, cos_p, sin_p, *proj_args, tm=tm_p, **proj_kw)
        sb_o = _sb_prompt(q, kb, vb, tri)
        lat = _mla_prompt(qcat, kcat_t, vext, tq=tq_mla, tk=tq_mla, kvlora=kvlora)
        y = moe_block(*_merge(xp.reshape(B * T, D), sb_o.reshape(B * T, sbw), lat.reshape(B * T, H * kvlora), *merge_args,
                              _strict_lower(tm_p), tm=tm_p))
        xp = y.reshape(B, T, D)
        for lst, a in zip(outs_p, (k, v, ckv, kpe)):
            lst.append(a)

        q2, k2, v2, kb2, vb2, ckv2, kpe2, kcat2, _, _, qcat2 = _proj(xs, cos_s, sin_s, *proj_args, tm=ns, **proj_kw)
        per_stream = lambda a: a.reshape(SBt, SQ, a.shape[-1])
        k_all = _pad_rows(jnp.concatenate([cache_sb_k[l].reshape(SBt, past, sbw).astype(BF16), per_stream(kb2)], axis=1), tk_all)
        v_all = _pad_rows(jnp.concatenate([cache_sb_v[l].reshape(SBt, past, sbw).astype(BF16), per_stream(vb2)], axis=1), tk_all)
        kc_past = jnp.concatenate([cache_mla_ckv[l], _pad_lanes(cache_mla_kpe[l])], axis=-1).astype(BF16)
        kcat_all = _pad_rows(jnp.concatenate([kc_past, per_stream(kcat2)], axis=1), tk_all)
        sb_o2 = _sb_sample(per_stream(q2), k_all, v_all, tri, qpos0=past)
        qcat_s = jnp.transpose(qcat2.reshape(H, SBt, SQ, kvlora + LANES), (1, 0, 2, 3))
        lat2 = _mla_sample(qcat_s, kcat_all, qpos0=past, nvalid=past + SQ, kvlora=kvlora)
        y2 = moe_block(*_merge(xs.reshape(ns, D), sb_o2.reshape(ns, sbw), lat2.reshape(ns, H * kvlora), *merge_args,
                              _strict_lower(ns), tm=ns))
        xs = y2.reshape(1, ns, D)
        for lst, a in zip(outs_s, (k2, v2, ckv2, kpe2)):
            lst.append(a)

    heads = lambda a, n: a.reshape(n, -1, SB_HEADS, sb_hd)
    return (xp, xs.reshape(SBt, SQ, D),
            jnp.stack([heads(a, B) for a in outs_p[0]]), jnp.stack([heads(a, B) for a in outs_p[1]]),
            jnp.stack(outs_p[2]), jnp.stack(outs_p[3]),
            jnp.stack([heads(a.reshape(SBt, SQ, sbw), SBt) for a in outs_s[0]]),
            jnp.stack([heads(a.reshape(SBt, SQ, sbw), SBt) for a in outs_s[1]]),
            jnp.stack([a.reshape(SBt, SQ, kvlora) for a in outs_s[2]]),
            jnp.stack([a.reshape(SBt, SQ, rope) for a in outs_s[3]]))
```
